```python
import math
import jax
import jax.numpy as jnp
from jax import lax
import numpy as np

D_MODEL = 2048
BATCH = 8
SEQ = 4096
DEPTH = 2

CTX_LEN = 256
GRID_W = 64
D_MIX = D_MODEL
A_HEAD_DIM = 128
A_WIDTH = D_MIX // 4
A_HEADS = A_WIDTH // A_HEAD_DIM
GMLP_CHUNK = 128
B_HEAD_DIM = 128
B_WIDTH = 3 * D_MIX // 8
B_HEADS = B_WIDTH // B_HEAD_DIM
HGRN_CHUNK = 64
C_HEAD_DIM = 64
C_WIDTH = D_MIX - A_WIDTH - B_WIDTH
C_HEADS = C_WIDTH // C_HEAD_DIM
DECAY_LORA = 64
AAA_LORA = 64
GATE_LORA = 128
A_COLS = 2 * A_WIDTH
B_COLS = 5 * B_WIDTH
C_COLS = 3 * C_WIDTH + DECAY_LORA + AAA_LORA + GATE_LORA
IN_COLS = A_COLS + B_COLS + C_COLS
C_SPLITS = (C_WIDTH, 2 * C_WIDTH, 3 * C_WIDTH, 3 * C_WIDTH + DECAY_LORA, 3 * C_WIDTH + DECAY_LORA + AAA_LORA)
N_EXPERTS = 16
N_EXPERT_GROUPS = 4
EXPERTS_PER_GROUP = N_EXPERTS // N_EXPERT_GROUPS
TOP_K = 2
D_EXPERT = D_MODEL // 2
NORM_EPS = 1e-6
RWKV_GN_EPS = 64e-5
DECAY_SCALE = math.exp(-0.5)

kernel_name = "hybrid_gmlp_hgrn2_rwkv7_moe_dit"


def rmsnorm(x, g):
    xf = x.astype(jnp.float32)
    y = xf * lax.rsqrt(jnp.mean(xf * xf, axis=-1, keepdims=True) + NORM_EPS)
    return (y * g.astype(jnp.float32)).astype(x.dtype)


def head_layernorm(x, g, b):
    xf = x.astype(jnp.float32)
    xc = xf - jnp.mean(xf, axis=-1, keepdims=True)
    var = jnp.mean(xc * xc, axis=-1, keepdims=True)
    return xc * lax.rsqrt(var + RWKV_GN_EPS) * g.astype(jnp.float32) + b.astype(jnp.float32)


def gmlp_spatial_gating(cols, norm_g, ws, wb):
    bsz, t, _ = cols.shape
    u, v = jnp.split(jax.nn.gelu(cols, approximate=False), 2, axis=-1)
    v = rmsnorm(v.reshape(bsz, t, A_HEADS, A_HEAD_DIM), norm_g.reshape(A_HEADS, A_HEAD_DIM))
    v = v.reshape(bsz, t // GMLP_CHUNK, GMLP_CHUNK, A_HEADS, A_HEAD_DIM)
    s = jnp.einsum('hts,bnshc->bnthc', ws, v) + wb.T[:, :, None]
    return u * s.reshape(bsz, t, A_WIDTH)


def to_dir_chunks(fwd, bwd):
    s = jnp.stack([fwd, bwd], axis=0)
    z, bsz, t, h, d = s.shape
    return s.reshape(z, bsz, t // HGRN_CHUNK, HGRN_CHUNK, h, d).transpose(2, 0, 1, 4, 3, 5)


def from_dir_chunks(ys):
    n, z, bsz, h, c, d = ys.shape
    return ys.transpose(1, 2, 0, 4, 3, 5).reshape(z, bsz, n * c, h, d)


def hgrn2_chunk_step(state, inp):
    q, k, v, log_f = inp
    cl = q.shape[-2]
    lower = jnp.tril(jnp.ones((cl, cl), dtype=bool))[:, :, None]
    b = jnp.cumsum(log_f, axis=-2)
    diff = b[..., :, None, :] - b[..., None, :, :]
    decay = jnp.where(lower, jnp.exp(jnp.where(lower, diff, 0.0)), 0.0)
    scores = jnp.einsum('zbhtd,zbhsd,zbhtsd->zbhts', q, k, decay)
    out = jnp.einsum('zbhts,zbhsv->zbhtv', scores, v) + jnp.einsum('zbhtd,zbhdv->zbhtv', q * jnp.exp(b), state)
    b_last = b[..., -1:, :]
    state = jnp.exp(b_last)[..., 0, :, None] * state + jnp.einsum('zbhsd,zbhsv->zbhdv', k * jnp.exp(b_last - b), v)
    return state, out


def hgrn2_mixer(cols, lb, norm_g, state0):
    bsz, t, _ = cols.shape
    heads = lambda y: y.reshape(bsz, t, B_HEADS, B_HEAD_DIM)
    q, i, zf, zb, og = jnp.split(cols.astype(jnp.float32), 5, axis=-1)
    q = heads(jax.nn.silu(q))
    v = heads(i)
    lb = lb.astype(jnp.float32).reshape(2, B_HEADS, B_HEAD_DIM)

    def forget(z, lbd):
        z = heads(z)
        log_f = jnp.logaddexp(jnp.log(lbd), jnp.log1p(-lbd) + jax.nn.log_sigmoid(z))
        return log_f, (1.0 - lbd) * jax.nn.sigmoid(-z)

    lf_f, k_f = forget(zf, lb[0])
    lf_b, k_b = forget(zb, lb[1])
    rev = lambda y: y[:, ::-1]
    xs = (to_dir_chunks(q, rev(q)), to_dir_chunks(k_f, rev(k_b)), to_dir_chunks(v, rev(v)), to_dir_chunks(lf_f, rev(lf_b)))
    state, ys = lax.scan(hgrn2_chunk_step, state0, xs)
    y = from_dir_chunks(ys)
    o = y[0] + rev(y[1])
    o = rmsnorm(o, norm_g.reshape(B_HEADS, B_HEAD_DIM)).reshape(bsz, t, B_WIDTH) * jax.nn.sigmoid(og)
    return o.astype(cols.dtype), state


def shift_grid(z):
    bsz, t, c = z.shape
    rows = t // GRID_W
    g = z.reshape(bsz, rows, GRID_W, c // 4, 4)
    left = jnp.pad(g[:, :, :-1, :, 0], ((0, 0), (0, 0), (1, 0), (0, 0)))
    right = jnp.pad(g[:, :, 1:, :, 1], ((0, 0), (0, 0), (0, 1), (0, 0)))
    up = jnp.pad(g[:, :-1, :, :, 2], ((0, 0), (1, 0), (0, 0), (0, 0)))
    down = jnp.pad(g[:, 1:, :, :, 3], ((0, 0), (0, 1), (0, 0), (0, 0)))
    return jnp.stack([left, right, up, down], axis=-1).reshape(bsz, t, c)


def shift_seq(z):
    bsz, t, c = z.shape
    g = z.reshape(bsz, t, c // 4, 4)
    prev = jnp.pad(g[:, :-1], ((0, 0), (1, 0), (0, 0), (0, 0)))
    nxt = jnp.pad(g[:, 1:], ((0, 0), (0, 1), (0, 0), (0, 0)))
    return jnp.stack([prev[..., 0], nxt[..., 1], prev[..., 2], nxt[..., 3]], axis=-1).reshape(bsz, t, c)


def rwkv7_step(state, inp):
    r, w, k, v, kk, a = inp
    sa = jnp.einsum('zbhvk,zbhk->zbhv', state, kk)
    state = state * w[..., None, :] - sa[..., :, None] * (kk * a)[..., None, :] + v[..., :, None] * k[..., None, :]
    return state, jnp.einsum('zbhvk,zbhk->zbhv', state, r)


def rwkv7_mixer(cols, shift_fn, mu, w0, w_up, a0, a_up, g_up, k_k, k_a, r_k, gn_g, gn_b, state0):
    bsz, t, _ = cols.shape
    z = cols.astype(jnp.float32)
    z = z + mu * (shift_fn(z) - z)
    r, k, v, wd, ad, gd = jnp.split(z, C_SPLITS, axis=-1)
    heads = lambda y: y.reshape(y.shape[:-1] + (C_HEADS, C_HEAD_DIM))
    decay = jnp.exp(-DECAY_SCALE * jax.nn.sigmoid(w0[:, None, None, :] + jnp.einsum('btr,zrc->zbtc', jnp.tanh(wd), w_up)))
    a = jax.nn.sigmoid(a0 + ad @ a_up)
    g = jax.nn.sigmoid(gd) @ g_up
    kk = heads(k * k_k)
    kk = kk / jnp.maximum(jnp.sqrt(jnp.sum(kk * kk, axis=-1, keepdims=True)), 1e-12)
    k2 = heads(k * (1.0 + (a - 1.0) * k_a))
    r, v, a, decay = heads(r), heads(v), heads(a), heads(decay)

    def to_time(fwd, bwd):
        return jnp.stack([fwd, bwd[:, ::-1]], axis=0).transpose(2, 0, 1, 3, 4)

    xs = (to_time(r, r), to_time(decay[0], decay[1]), to_time(k2, k2), to_time(v, v), to_time(kk, kk), to_time(a, a))
    state, ys = lax.scan(rwkv7_step, state0, xs)
    o = ys[:, 0].transpose(1, 0, 2, 3) + ys[::-1, 1].transpose(1, 0, 2, 3)
    o = head_layernorm(o, heads(gn_g), heads(gn_b))
    bonus = jnp.sum(r * k2 * heads(r_k), axis=-1, keepdims=True) * v
    y = (o + bonus).reshape(bsz, t, C_WIDTH) * g
    return y.astype(cols.dtype), state


def grouped_moe(h, router_w, router_b, w_gate_up, w_down):
    n = h.shape[0]
    logits = jnp.dot(h, router_w).astype(jnp.float32) + router_b.astype(jnp.float32)
    probs = jax.nn.softmax(logits, axis=-1)
    group_score = lax.top_k(probs.reshape(n, N_EXPERT_GROUPS, EXPERTS_PER_GROUP), TOP_K)[0].sum(-1)
    sel = jnp.argmax(group_score, axis=-1)
    in_group = (jnp.arange(N_EXPERTS) // EXPERTS_PER_GROUP)[None, :] == sel[:, None]
    top_p, top_i = lax.top_k(jnp.where(in_group, probs, -1.0), TOP_K)
    top_w = top_p / jnp.sum(top_p, axis=-1, keepdims=True)
    gates = jnp.sum(jax.nn.one_hot(top_i, N_EXPERTS, dtype=jnp.float32) * top_w[..., None], axis=1).astype(h.dtype)
    y = jnp.zeros_like(h)
    for e in range(N_EXPERTS):
        gt, up = jnp.split(h @ w_gate_up[e], 2, axis=-1)
        y = y + gates[:, e:e + 1] * ((jax.nn.silu(gt) * up) @ w_down[e])
    return y


def setup_inputs(seed: int = 0) -> dict:
    key = jax.random.key(seed)
    keys = list(jax.random.split(key, 32))

    def nrm(shape, scale):
        return jax.random.normal(keys.pop(), shape, jnp.float32) * scale

    d = D_MODEL
    return {
        'x': nrm((BATCH, SEQ, d), 1.0),
        'c': nrm((BATCH, d), 1.0),
        'ctx': nrm((BATCH, CTX_LEN, d), 1.0),
        'c_ctx': nrm((d,), 1.0),
        'ada_w': nrm((DEPTH, d, 6 * d), 0.5 * d ** -0.5),
        'ada_b': nrm((DEPTH, 6 * d), 0.02),
        'norm1_g': 1.0 + nrm((DEPTH, d), 0.02),
        'norm2_g': 1.0 + nrm((DEPTH, d), 0.02),
        'w_in': nrm((DEPTH, d, IN_COLS), d ** -0.5),
        'w_out': nrm((DEPTH, D_MIX, d), D_MIX ** -0.5),
        'gmlp_norm_g': 1.0 + nrm((DEPTH, A_WIDTH), 0.02),
        'gmlp_ws': nrm((DEPTH, A_HEADS, GMLP_CHUNK, GMLP_CHUNK), GMLP_CHUNK ** -0.5),
        'gmlp_b': 1.0 + nrm((DEPTH, A_HEADS, GMLP_CHUNK), 0.1),
        'hgrn_lb_logits': nrm((2, DEPTH, B_WIDTH), 1.0),
        'hgrn_norm_g': 1.0 + nrm((DEPTH, B_WIDTH), 0.02),
        'rwkv_mu': jax.random.uniform(keys.pop(), (DEPTH, C_COLS), jnp.float32),
        'rwkv_w0': nrm((DEPTH, 2, C_WIDTH), 1.0),
        'rwkv_w_up': nrm((DEPTH, 2, DECAY_LORA, C_WIDTH), DECAY_LORA ** -0.5),
        'rwkv_a0': nrm((DEPTH, C_WIDTH), 0.5),
        'rwkv_a_up': nrm((DEPTH, AAA_LORA, C_WIDTH), AAA_LORA ** -0.5),
        'rwkv_g_up': nrm((DEPTH, GATE_LORA, C_WIDTH), GATE_LORA ** -0.5),
        'rwkv_k_k': 0.85 + nrm((DEPTH, C_WIDTH), 0.05),
        'rwkv_k_a': 1.0 + nrm((DEPTH, C_WIDTH), 0.05),
        'rwkv_r_k': nrm((DEPTH, C_WIDTH), 0.1),
        'rwkv_gn_g': 1.0 + nrm((DEPTH, C_WIDTH), 0.02),
        'rwkv_gn_b': nrm((DEPTH, C_WIDTH), 0.02),
        'router_w': nrm((d, N_EXPERTS), d ** -0.5),
        'router_b': nrm((N_EXPERTS,), 0.01),
        'moe_w_gate_up': nrm((DEPTH, N_EXPERTS, d, 2 * D_EXPERT), d ** -0.5),
        'moe_w_down': nrm((DEPTH, N_EXPERTS, D_EXPERT, d), D_EXPERT ** -0.5),
        'final_norm_g': 1.0 + nrm((d,), 0.02),
    }


def reference(x, c, ctx, c_ctx, ada_w, ada_b, norm1_g, norm2_g, w_in, w_out, gmlp_norm_g, gmlp_ws, gmlp_b,
              hgrn_lb_logits, hgrn_norm_g, rwkv_mu, rwkv_w0, rwkv_w_up, rwkv_a0, rwkv_a_up, rwkv_g_up,
              rwkv_k_k, rwkv_k_a, rwkv_r_k, rwkv_gn_g, rwkv_gn_b, router_w, router_b, moe_w_gate_up,
              moe_w_down, final_norm_g):
    bsz, t, d = x.shape
    ctx_len = ctx.shape[1]
    x_lat, x_ctx = x, ctx
    lbs = jnp.cumsum(jax.nn.softmax(hgrn_lb_logits.astype(jnp.float32), axis=1), axis=1)
    lbs = lbs - lbs[:, :1]
    hgrn_s0 = jnp.zeros((2, bsz, B_HEADS, B_HEAD_DIM, B_HEAD_DIM), jnp.float32)
    rwkv_s0 = jnp.zeros((2, bsz, C_HEADS, C_HEAD_DIM, C_HEAD_DIM), jnp.float32)
    for l in range(DEPTH):
        last = l == DEPTH - 1
        m_lat = jax.nn.silu(c) @ ada_w[l] + ada_b[l]
        m_ctx = jax.nn.silu(c_ctx) @ ada_w[l] + ada_b[l]
        sh1, sc1, g1, sh2, sc2, g2 = jnp.split(m_lat[:, None, :], 6, axis=-1)
        sh1c, sc1c, g1c, sh2c, sc2c, g2c = jnp.split(m_ctx, 6)
        h_lat = rmsnorm(x_lat, norm1_g[l]) * (1.0 + sc1) + sh1
        h_ctx = rmsnorm(x_ctx, norm1_g[l]) * (1.0 + sc1c) + sh1c
        pa_lat, pb_lat, pc_lat = jnp.split(h_lat @ w_in[l], (A_COLS, A_COLS + B_COLS), axis=-1)
        pa_ctx, pb_ctx, pc_ctx = jnp.split(h_ctx @ w_in[l], (A_COLS, A_COLS + B_COLS), axis=-1)
        ya_lat = gmlp_spatial_gating(pa_lat, gmlp_norm_g[l], gmlp_ws[l], gmlp_b[l])
        yb_ctx, hgrn_state = hgrn2_mixer(pb_ctx, lbs[:, l], hgrn_norm_g[l], hgrn_s0)
        yb_lat, _ = hgrn2_mixer(pb_lat, lbs[:, l], hgrn_norm_g[l], hgrn_state)
        rw = (rwkv_mu[l], rwkv_w0[l], rwkv_w_up[l], rwkv_a0[l], rwkv_a_up[l], rwkv_g_up[l],
              rwkv_k_k[l], rwkv_k_a[l], rwkv_r_k[l], rwkv_gn_g[l], rwkv_gn_b[l])
        yc_ctx, rwkv_state = rwkv7_mixer(pc_ctx, shift_seq, *rw, rwkv_s0)
        yc_lat, _ = rwkv7_mixer(pc_lat, shift_grid, *rw, rwkv_state)
        y_lat = jnp.concatenate([ya_lat, yb_lat, yc_lat], axis=-1) @ w_out[l]
        x_lat = x_lat + g1 * y_lat
        h2_lat = rmsnorm(x_lat, norm2_g[l]) * (1.0 + sc2) + sh2
        if last:
            f_lat = grouped_moe(h2_lat.reshape(-1, d), router_w, router_b, moe_w_gate_up[l], moe_w_down[l])
            x_lat = x_lat + g2 * f_lat.reshape(bsz, t, d)
        else:
            ya_ctx = gmlp_spatial_gating(pa_ctx, gmlp_norm_g[l], gmlp_ws[l], gmlp_b[l])
            y_ctx = jnp.concatenate([ya_ctx, yb_ctx, yc_ctx], axis=-1) @ w_out[l]
            x_ctx = x_ctx + g1c * y_ctx
            h2_ctx = rmsnorm(x_ctx, norm2_g[l]) * (1.0 + sc2c) + sh2c
            tokens = jnp.concatenate([h2_ctx.reshape(-1, d), h2_lat.reshape(-1, d)], axis=0)
            f = grouped_moe(tokens, router_w, router_b, moe_w_gate_up[l], moe_w_down[l])
            n_ctx = bsz * ctx_len
            x_ctx = x_ctx + g2c * f[:n_ctx].reshape(bsz, ctx_len, d)
            x_lat = x_lat + g2 * f[n_ctx:].reshape(bsz, t, d)
    return rmsnorm(x_lat, final_norm_g)
```

```python
import functools
import math

import jax
import jax.numpy as jnp
from jax import lax
from jax.experimental import pallas as pl
from jax.experimental.pallas import tpu as pltpu

F32 = jnp.float32
BF16 = jnp.bfloat16

GRID_W = 64
A_HEAD_DIM = 128
B_HEAD_DIM = 128
C_HEAD_DIM = 64
GMLP_CHUNK = 128
DECAY_LORA = 64
AAA_LORA = 64
GATE_LORA = 128
N_EXPERTS = 16
N_EXPERT_GROUPS = 4
NORM_EPS = 1e-6
RWKV_GN_EPS = 64e-5
DECAY_SCALE = math.exp(-0.5)

LANES = 128
SCAN_CHUNK = 64
HGRN_SUB = 16
TM_MATMUL = 512
TM_TOKEN = 256
VMEM_LIMIT = 56 * 1024 * 1024

NN = (((1,), (0,)), ((), ()))
NT = (((1,), (1,)), ((), ()))
TN = (((0,), (0,)), ((), ()))


def _dot(a, b, dims=NN):
    return lax.dot_general(a, b, dims, preferred_element_type=F32)


def _mm1(a, b, dims=NN):
    return _dot(a.astype(BF16), b.astype(BF16), dims)


def _split2(x):
    hi = x.astype(BF16)
    lo = (x - hi.astype(F32)).astype(BF16)
    return hi, lo


def _split3(x):
    hi = x.astype(BF16)
    r1 = x - hi.astype(F32)
    mid = r1.astype(BF16)
    lo = (r1 - mid.astype(F32)).astype(BF16)
    return hi, mid, lo


def _mm3(a, b, dims=NN):
    ah, al = _split2(a)
    bh, bl = _split2(b)
    return _dot(ah, bh, dims) + (_dot(al, bh, dims) + _dot(ah, bl, dims))


def _mm_exact_lhs(a_bf16, b, dims=NN):
    h, m, l = _split3(b)
    return _dot(a_bf16, h, dims) + (_dot(a_bf16, m, dims) + _dot(a_bf16, l, dims))


def _mm_exact_rhs(a, b_bf16, dims=NN):
    h, m, l = _split3(a)
    return _dot(h, b_bf16, dims) + (_dot(m, b_bf16, dims) + _dot(l, b_bf16, dims))


def _sigmoid(x):
    return jax.nn.sigmoid(x)


def _rms(x):
    return x * lax.rsqrt(jnp.mean(x * x, axis=-1, keepdims=True) + NORM_EPS)


def _params(*sem):
    return pltpu.CompilerParams(dimension_semantics=sem, vmem_limit_bytes=VMEM_LIMIT)


def _adaln_kernel(c_ref, w_ref, b_ref, o_ref):
    c = c_ref[...]
    o_ref[...] = _mm3(c * _sigmoid(c), w_ref[...]) + b_ref[...]


def _adaln(c_rows, ada_w, ada_b):
    depth, d, n = ada_w.shape
    rows = c_rows.shape[0]
    tn = 1024
    return pl.pallas_call(
        _adaln_kernel,
        grid=(depth, n // tn),
        in_specs=[
            pl.BlockSpec((rows, d), lambda l, j: (0, 0)),
            pl.BlockSpec((None, d, tn), lambda l, j: (l, 0, j)),
            pl.BlockSpec((None, 1, tn), lambda l, j: (l, 0, j)),
        ],
        out_specs=pl.BlockSpec((None, rows, tn), lambda l, j: (l, 0, j)),
        out_shape=jax.ShapeDtypeStruct((depth, rows, n), F32),
        compiler_params=_params("parallel", "parallel"),
        name="adaln",
    )(c_rows, ada_w, ada_b.reshape(depth, 1, n))


class _Layout:
    def __init__(self, bsz, tc, tl):
        self.bsz, self.tc, self.tl = bsz, tc, tl
        self.n_ctx = bsz * tc
        self.n_lat = bsz * tl
        self.n = self.n_ctx + self.n_lat

    def mod_row(self, tile, tm):
        ctx_tiles = self.n_ctx // tm
        return jnp.where(tile < ctx_tiles, self.bsz, (tile - ctx_tiles) // (self.tl // tm))

    def scan_block(self, b, step, reverse):
        ncc = self.tc // SCAN_CHUNK
        ncl = self.tl // SCAN_CHUNK
        if reverse:
            mc = ncc - 1 - step
            ml = ncl - 1 - (step - ncc)
        else:
            mc = step
            ml = step - ncc
        return jnp.where(step < ncc, b * ncc + mc, self.bsz * ncc + b * ncl + ml)


def _mod_spec(lay, tm, layer, part, d, tile_off, grid_pos):
    def index(*ids):
        return (layer, lay.mod_row(ids[grid_pos] + tile_off, tm), part, 0, 0)

    return pl.BlockSpec((None, None, None, 1, d), index)


def _inproj_kernel(x_ref, g_ref, sc_ref, sh_ref, w_ref, o_ref):
    h = _rms(x_ref[...]) * g_ref[...] * (1.0 + sc_ref[...]) + sh_ref[...]
    o_ref[...] = _mm1(h, w_ref[...])


def _inproj(x_all, norm_g, mods, layer, w_bf16, tn, lay):
    n, d = x_all.shape
    cols = w_bf16.shape[1]
    tm = TM_MATMUL
    return pl.pallas_call(
        _inproj_kernel,
        grid=(cols // tn, n // tm),
        in_specs=[
            pl.BlockSpec((tm, d), lambda j, i: (i, 0)),
            pl.BlockSpec((1, d), lambda j, i: (0, 0)),
            _mod_spec(lay, tm, layer, 1, d, 0, 1),
            _mod_spec(lay, tm, layer, 0, d, 0, 1),
            pl.BlockSpec((d, tn), lambda j, i: (0, j)),
        ],
        out_specs=pl.BlockSpec((tm, tn), lambda j, i: (i, j)),
        out_shape=jax.ShapeDtypeStruct((n, cols), F32),
        compiler_params=_params("parallel", "parallel"),
        name="inproj",
    )(x_all, norm_g.reshape(1, d), mods, mods, w_bf16)


def _gmlp_kernel(p_ref, g_ref, ws_ref, wbt_ref, o_ref, *, heads):
    x = p_ref[...]
    ge = 0.5 * x * (1.0 + lax.erf(x * (1.0 / math.sqrt(2.0))))
    width = heads * A_HEAD_DIM
    tm = x.shape[0]
    for h in range(heads):
        lo, hi = h * A_HEAD_DIM, (h + 1) * A_HEAD_DIM
        vn = _rms(ge[:, width + lo:width + hi]) * g_ref[:, lo:hi]
        for c in range(tm // GMLP_CHUNK):
            r0, r1 = c * GMLP_CHUNK, (c + 1) * GMLP_CHUNK
            s = _mm1(ws_ref[h], vn[r0:r1]) + wbt_ref[:, h:h + 1]
            o_ref[r0:r1, lo:hi] = ge[r0:r1, lo:hi] * s


def _gmlp(pa, norm_g, ws, wb, tile_off, n_tiles):
    n, cols = pa.shape
    heads = ws.shape[0]
    width = cols // 2
    tm = TM_TOKEN
    return pl.pallas_call(
        functools.partial(_gmlp_kernel, heads=heads),
        grid=(n_tiles,),
        in_specs=[
            pl.BlockSpec((tm, cols), lambda i: (i + tile_off, 0)),
            pl.BlockSpec((1, width), lambda i: (0, 0)),
            pl.BlockSpec((heads, GMLP_CHUNK, GMLP_CHUNK), lambda i: (0, 0, 0)),
            pl.BlockSpec((GMLP_CHUNK, heads), lambda i: (0, 0)),
        ],
        out_specs=pl.BlockSpec((tm, width), lambda i: (i, 0)),
        out_shape=jax.ShapeDtypeStruct((n_tiles * tm, width), F32),
        compiler_params=_params("parallel"),
        name="gmlp",
    )(pa, norm_g.reshape(1, width), ws.astype(BF16), wb.T)


def _hgrn_kernel(q_ref, i_ref, z_ref, lb_ref, y_ref, st_ref, *, reverse, heads):
    @pl.when(pl.program_id(1) == 0)
    def _():
        st_ref[...] = jnp.zeros_like(st_ref)

    cl, sub, hd = SCAN_CHUNK, HGRN_SUB, B_HEAD_DIM
    q = q_ref[...]
    q = q * _sigmoid(q)
    v = i_ref[...]
    z = z_ref[...]
    lb = lb_ref[...]
    t_lb = jnp.log(lb)
    t_z = jnp.log1p(-lb) + (jnp.minimum(z, 0.0) - jnp.log1p(jnp.exp(-jnp.abs(z))))
    lf = jnp.maximum(t_lb, t_z) + jnp.log1p(jnp.exp(-jnp.abs(t_lb - t_z)))
    k = (1.0 - lb) * _sigmoid(-z)

    row = lax.broadcasted_iota(jnp.int32, (cl, cl), 0)
    col = lax.broadcasted_iota(jnp.int32, (cl, cl), 1)
    tri = (col >= row) if reverse else (col <= row)
    c = _mm_exact_lhs(tri.astype(BF16), lf)
    total = c[0:1] if reverse else c[cl - 1:cl]
    qe = q * jnp.exp(c)
    ke = k * jnp.exp(total - c)

    inter = []
    for h in range(heads):
        lo, hi = h * hd, (h + 1) * hd
        inter.append(_mm1(qe[:, lo:hi], st_ref[h], NT))

    for i in range(cl // sub):
        r0, r1 = i * sub, (i + 1) * sub
        if reverse:
            k0, k1 = r0, cl
            ref = c[r1:r1 + 1] if r1 < cl else jnp.zeros_like(total)
        else:
            k0, k1 = 0, r1
            ref = c[r0 - 1:r0] if r0 > 0 else jnp.zeros_like(total)
        qt = q[r0:r1] * jnp.exp(c[r0:r1] - ref)
        kt = k[k0:k1] * jnp.exp(ref - c[k0:k1])
        qrow = lax.broadcasted_iota(jnp.int32, (sub, k1 - k0), 0) + r0
        kcol = lax.broadcasted_iota(jnp.int32, (sub, k1 - k0), 1) + k0
        keep = (kcol >= qrow) if reverse else (kcol <= qrow)
        for h in range(heads):
            lo, hi = h * hd, (h + 1) * hd
            sc = jnp.where(keep, _mm1(qt[:, lo:hi], kt[:, lo:hi], NT), 0.0)
            y_ref[r0:r1, lo:hi] = inter[h][r0:r1] + _mm1(sc, v[k0:k1, lo:hi])

    for h in range(heads):
        lo, hi = h * hd, (h + 1) * hd
        st_ref[h] = st_ref[h] * jnp.exp(total[:, lo:hi]) + _mm1(v[:, lo:hi], ke[:, lo:hi], TN)


def _hgrn(pb, lb_dir, lay, reverse):
    n = pb.shape[0]
    width = pb.shape[1] // 5
    heads = width // B_HEAD_DIM
    steps = (lay.tc + lay.tl) // SCAN_CHUNK
    blk = lambda comp: pl.BlockSpec(
        (SCAN_CHUNK, width), lambda b, s: (lay.scan_block(b, s, reverse), comp))
    return pl.pallas_call(
        functools.partial(_hgrn_kernel, reverse=reverse, heads=heads),
        grid=(lay.bsz, steps),
        in_specs=[blk(0), blk(1), blk(3 if reverse else 2),
                  pl.BlockSpec((1, width), lambda b, s: (0, 0))],
        out_specs=blk(0),
        out_shape=jax.ShapeDtypeStruct((n, width), F32),
        scratch_shapes=[pltpu.VMEM((heads, B_HEAD_DIM, B_HEAD_DIM), F32)],
        compiler_params=_params("parallel", "arbitrary"),
        name="hgrn_bwd" if reverse else "hgrn_fwd",
    )(pb, pb, pb, lb_dir.reshape(1, width))


def _rwkv_prep_kernel(cur_ref, prev_ref, next_ref, mu_ref, wl_ref, gup_ref, w0f_ref, w0b_ref, a0_ref,
                      kk_ref, ka_ref, rk_ref, seg_ref,
                      r_out, lwf_out, lwb_out, k_out, v_out, kk_out, ab_out, g_out, bonus_out,
                      *, ctx_tiles, rows_per_batch, width):
    tile = pl.program_id(0)
    z = cur_ref[...]
    tm, cols = z.shape
    halo = prev_ref.shape[0]
    ext = jnp.concatenate([prev_ref[...], z, next_ref[...]], axis=0)
    left = ext[halo - 1:halo - 1 + tm]
    right = ext[halo + 1:halo + 1 + tm]
    up = ext[0:tm]
    down = ext[2 * halo:2 * halo + tm]

    is_ctx = tile < ctx_tiles
    t = lax.broadcasted_iota(jnp.int32, (tm, 1), 0)
    rowb = t + ((tile - ctx_tiles) * tm) % rows_per_batch
    line = jnp.where(is_ctx, tm, GRID_W)
    pos = t & (line - 1)
    left = jnp.where(pos != 0, left, 0.0)
    right = jnp.where(pos != line - 1, right, 0.0)
    up = jnp.where(rowb >= GRID_W, up, 0.0)
    down = jnp.where(rowb < rows_per_batch - GRID_W, down, 0.0)
    slot = lax.broadcasted_iota(jnp.int32, (1, cols), 1) % 4
    shifted = jnp.where(
        slot == 0, left,
        jnp.where(slot == 1, right,
                  jnp.where(slot == 2, jnp.where(is_ctx, left, up), jnp.where(is_ctx, right, down))))
    xx = z + mu_ref[...] * (shifted - z)

    r = xx[:, 0:width]
    k = xx[:, width:2 * width]
    v = xx[:, 2 * width:3 * width]
    lora = xx[:, 3 * width:3 * width + DECAY_LORA + AAA_LORA]
    lane = lax.broadcasted_iota(jnp.int32, (1, DECAY_LORA + AAA_LORA), 1)
    lora = jnp.where(lane < DECAY_LORA, jnp.tanh(lora), lora)
    up3 = _mm3(lora, wl_ref[...])
    gd = xx[:, 3 * width + DECAY_LORA + AAA_LORA:]
    g_out[...] = _mm3(_sigmoid(gd), gup_ref[...])
    lwf_out[...] = -DECAY_SCALE * _sigmoid(w0f_ref[...] + up3[:, 0:width])
    lwb_out[...] = -DECAY_SCALE * _sigmoid(w0b_ref[...] + up3[:, width:2 * width])
    a = _sigmoid(a0_ref[...] + up3[:, 2 * width:3 * width])

    seg = seg_ref[...]
    kk = k * kk_ref[...]
    kk = kk / jnp.maximum(jnp.sqrt(_mm_exact_rhs(kk * kk, seg)), 1e-12)
    k2 = k * (1.0 + (a - 1.0) * ka_ref[...])
    r_out[...] = r
    k_out[...] = k2
    v_out[...] = v
    kk_out[...] = kk
    ab_out[...] = kk * a
    bonus_out[...] = _mm_exact_rhs(r * k2 * rk_ref[...], seg) * v


def _rwkv_prep(pc, lay, mu, w_lora, g_up, w0, a0, k_k, k_a, r_k):
    n, cols = pc.shape
    width = a0.shape[-1]
    tm = TM_TOKEN
    assert lay.tc == tm and lay.tl % tm == 0 and tm % GRID_W == 0
    halo = GRID_W
    hb = tm // halo
    last = n // halo - 1
    row = lambda a: a.reshape(1, -1)
    hi = lax.broadcasted_iota(jnp.int32, (width, width), 0) // C_HEAD_DIM
    hj = lax.broadcasted_iota(jnp.int32, (width, width), 1) // C_HEAD_DIM
    seg = (hi == hj).astype(BF16)
    full = lambda a: pl.BlockSpec(a.shape, lambda i: (0,) * a.ndim)
    consts = [row(mu), w_lora, g_up, row(w0[0]), row(w0[1]), row(a0), row(k_k), row(k_a), row(r_k), seg]
    out_spec = pl.BlockSpec((tm, width), lambda i: (i, 0))
    return pl.pallas_call(
        functools.partial(_rwkv_prep_kernel, ctx_tiles=lay.n_ctx // tm, rows_per_batch=lay.tl, width=width),
        grid=(n // tm,),
        in_specs=[
            pl.BlockSpec((tm, cols), lambda i: (i, 0)),
            pl.BlockSpec((halo, cols), lambda i: (jnp.maximum(i * hb - 1, 0), 0)),
            pl.BlockSpec((halo, cols), lambda i: (jnp.minimum((i + 1) * hb, last), 0)),
        ] + [full(a) for a in consts],
        out_specs=[out_spec] * 9,
        out_shape=[jax.ShapeDtypeStruct((n, width), F32)] * 9,
        compiler_params=_params("parallel"),
        name="rwkv_prep",
    )(pc, pc, pc, *consts)


def _rwkv_kernel(r_ref, lw_ref, k_ref, v_ref, kk_ref, ab_ref, o_ref, st_ref, *, reverse):
    @pl.when(pl.program_id(2) == 0)
    def _():
        st_ref[...] = jnp.zeros_like(st_ref)

    cl, hd = SCAN_CHUNK, C_HEAD_DIM
    r, lw, k, v, kk, ab = (ref[...] for ref in (r_ref, lw_ref, k_ref, v_ref, kk_ref, ab_ref))
    row = lax.broadcasted_iota(jnp.int32, (cl, cl), 0)
    col = lax.broadcasted_iota(jnp.int32, (cl, cl), 1)
    tri = (col >= row) if reverse else (col <= row)
    c = _mm_exact_lhs(tri.astype(BF16), lw)
    total = c[0:1] if reverse else c[cl - 1:cl]
    e_in = jnp.exp(c)
    e_out = jnp.exp(-c)
    kkg = kk * jnp.exp(c - lw)
    rg = r * e_in
    kd = k * e_out
    bd = ab * e_out
    e_last = jnp.exp(total - c)
    kc = k * e_last
    bc = ab * e_last

    lane = lax.broadcasted_iota(jnp.int32, (cl, 2 * hd), 1)
    m1 = (lane < hd).astype(F32)
    m2 = 1.0 - m1
    g1 = _mm3(jnp.concatenate([kkg * m1, rg * m1], 0), jnp.concatenate([bd, kd], 0), NT)
    g2 = _mm3(jnp.concatenate([kkg * m2, rg * m2], 0), jnp.concatenate([kd, bd], 0), NT)

    r2 = lax.broadcasted_iota(jnp.int32, (2 * cl, 2 * cl), 0)
    c2 = lax.broadcasted_iota(jnp.int32, (2 * cl, 2 * cl), 1)
    tt, ss = r2 % cl, c2 % cl
    strict = (ss > tt) if reverse else (ss < tt)
    same = (r2 // cl) == (c2 // cl)
    gl = jnp.where(strict, jnp.concatenate([g1[0:cl], g2[0:cl]], 0), 0.0)
    l_diag = jnp.where(same, gl, 0.0)
    l_anti = jnp.where(same, 0.0, gl)

    inv = jnp.where(r2 == c2, 1.0, 0.0) - l_diag
    pw = _mm3(l_diag, l_diag)
    n_sq = int(math.log2(cl)) - 1
    for i in range(n_sq):
        inv = inv + _mm3(inv, pw)
        if i + 1 < n_sq:
            pw = _mm3(pw, pw)

    s0 = st_ref[...]
    xs = _mm3(jnp.concatenate([kkg, rg], 0), s0, NT)
    lkv = _mm3(l_anti, jnp.concatenate([v * m2, v * m1], 0))
    rhs = xs[0:cl] + lkv[0:cl] + lkv[cl:]
    ust = _mm3(inv, jnp.concatenate([rhs * m1, rhs * m2], 0))
    u = ust[0:cl] + ust[cl:]

    ti = lax.broadcasted_iota(jnp.int32, (cl, 2 * cl), 0)
    si = lax.broadcasted_iota(jnp.int32, (cl, 2 * cl), 1) % cl
    incl = (si >= ti) if reverse else (si <= ti)
    ga1 = jnp.where(incl, g1[cl:], 0.0)
    ga2 = jnp.where(incl, g2[cl:], 0.0)
    o1 = _mm3(ga1, jnp.concatenate([-u * m1, v * m1], 0))
    o2 = _mm3(ga2, jnp.concatenate([v * m2, -u * m2], 0))
    o_ref[...] = xs[cl:] + o1 + o2

    upd = _mm3(jnp.concatenate([v, -u], 0), jnp.concatenate([kc, bc], 0), TN)
    st_ref[...] = s0 * jnp.exp(total) + jnp.where(same, upd, 0.0)


def _rwkv_scan(r, lw, k, v, kk, ab, lay, reverse):
    n, width = r.shape
    pairs = width // (2 * C_HEAD_DIM)
    steps = (lay.tc + lay.tl) // SCAN_CHUNK
    blk = pl.BlockSpec((SCAN_CHUNK, 2 * C_HEAD_DIM), lambda b, p, s: (lay.scan_block(b, s, reverse), p))
    return pl.pallas_call(
        functools.partial(_rwkv_kernel, reverse=reverse),
        grid=(lay.bsz, pairs, steps),
        in_specs=[blk] * 6,
        out_specs=blk,
        out_shape=jax.ShapeDtypeStruct((n, width), F32),
        scratch_shapes=[pltpu.VMEM((2 * C_HEAD_DIM, 2 * C_HEAD_DIM), F32)],
        compiler_params=_params("parallel", "parallel", "arbitrary"),
        name="rwkv_bwd" if reverse else "rwkv_fwd",
    )(r, lw, k, v, kk, ab)


def _outproj_kernel(x_ref, ya_ref, hf_ref, hb_ref, og_ref, hg_ref, rf_ref, rb_ref, bonus_ref, gate_ref,
                    gng_ref, gnb_ref, avg_ref, w_ref, g1_ref, n2_ref, sc2_ref, sh2_ref, rw_ref, rbias_ref,
                    xo_ref, h2_ref, lg_ref, *, a_width, b_width):
    o = hf_ref[...] + hb_ref[...]
    og = og_ref[...]
    yb = []
    for h in range(b_width // B_HEAD_DIM):
        lo, hi = h * B_HEAD_DIM, (h + 1) * B_HEAD_DIM
        yb.append(_rms(o[:, lo:hi]) * hg_ref[:, lo:hi] * _sigmoid(og[:, lo:hi]))
    yb = jnp.concatenate(yb, axis=1)

    rw = rf_ref[...] + rb_ref[...]
    avg = avg_ref[...]
    xc = rw - _mm_exact_rhs(rw, avg)
    var = _mm_exact_rhs(xc * xc, avg)
    yc = (xc * lax.rsqrt(var + RWKV_GN_EPS) * gng_ref[...] + gnb_ref[...] + bonus_ref[...]) * gate_ref[...]

    y = (_mm1(ya_ref[...], w_ref[0:a_width]) + _mm1(yb, w_ref[a_width:a_width + b_width])
         + _mm1(yc, w_ref[a_width + b_width:]))
    xn = x_ref[...] + g1_ref[...] * y
    xo_ref[...] = xn
    h2 = _rms(xn) * n2_ref[...] * (1.0 + sc2_ref[...]) + sh2_ref[...]
    h2_ref[...] = h2.astype(BF16)
    lg_ref[...] = _mm3(h2, rw_ref[...]) + rbias_ref[...]


def _outproj(x_all, ya, hf, hb, pb, hgrn_g, rf, rb, bonus, gate, gn_g, gn_b, w_out_bf16, mods, layer,
             norm2_g, router_w_pad, router_b_pad, lay, tile_off, n_tiles):
    n, d = x_all.shape
    a_width = ya.shape[1]
    b_width = hf.shape[1]
    c_width = rf.shape[1]
    tm = TM_TOKEN
    hi = lax.broadcasted_iota(jnp.int32, (c_width, c_width), 0) // C_HEAD_DIM
    hj = lax.broadcasted_iota(jnp.int32, (c_width, c_width), 1) // C_HEAD_DIM
    avg = jnp.where(hi == hj, 1.0 / C_HEAD_DIM, 0.0).astype(BF16)
    tok = lambda w, comp=0, off=tile_off: pl.BlockSpec((tm, w), lambda i: (i + off, comp))
    full = lambda a: pl.BlockSpec(a.shape, lambda i: (0,) * a.ndim)
    row = lambda a: a.reshape(1, -1)
    mod = lambda part: _mod_spec(lay, tm, layer, part, d, tile_off, 0)
    hg, gg, gb, n2 = row(hgrn_g), row(gn_g), row(gn_b), row(norm2_g)
    rows = n_tiles * tm
    return pl.pallas_call(
        functools.partial(_outproj_kernel, a_width=a_width, b_width=b_width),
        grid=(n_tiles,),
        in_specs=[tok(d), tok(a_width, off=0), tok(b_width), tok(b_width), tok(b_width, 4), full(hg),
                  tok(c_width), tok(c_width), tok(c_width), tok(c_width), full(gg), full(gb), full(avg),
                  full(w_out_bf16), mod(2), full(n2), mod(4), mod(3), full(router_w_pad), full(router_b_pad)],
        out_specs=[tok(d, off=0), tok(d, off=0), tok(LANES, off=0)],
        out_shape=[jax.ShapeDtypeStruct((rows, d), F32), jax.ShapeDtypeStruct((rows, d), BF16),
                   jax.ShapeDtypeStruct((rows, LANES), F32)],
        compiler_params=_params("parallel"),
        name="outproj",
    )(x_all, ya, hf, hb, pb, hg, rf, rb, bonus, gate, gg, gb, avg, w_out_bf16, mods, n2, mods, mods,
      router_w_pad, router_b_pad)


def _route(logits):
    per_group = N_EXPERTS // N_EXPERT_GROUPS
    le = [logits[:, e:e + 1] for e in range(N_EXPERTS)]
    mx = functools.reduce(jnp.maximum, le)
    ex = [jnp.exp(l - mx) for l in le]
    den = functools.reduce(jnp.add, ex)
    p = [e / den for e in ex]

    def top2_sum(vals):
        a, b, c, d = vals
        h1, l1, h2, l2 = jnp.maximum(a, b), jnp.minimum(a, b), jnp.maximum(c, d), jnp.minimum(c, d)
        return jnp.maximum(h1, h2) + jnp.maximum(jnp.minimum(h1, h2), jnp.maximum(l1, l2))

    assert per_group == 4
    score = [top2_sum(p[g * per_group:(g + 1) * per_group]) for g in range(N_EXPERT_GROUPS)]
    best, sel = score[0], jnp.zeros_like(score[0], dtype=jnp.int32)
    for g in range(1, N_EXPERT_GROUPS):
        better = score[g] > best
        best = jnp.where(better, score[g], best)
        sel = jnp.where(better, g, sel)
    pm = [jnp.where(sel == e // per_group, p[e], -1.0) for e in range(N_EXPERTS)]

    def first_max(vals):
        top = functools.reduce(jnp.maximum, vals)
        idx = jnp.full_like(sel, N_EXPERTS)
        for e in reversed(range(N_EXPERTS)):
            idx = jnp.where(vals[e] == top, e, idx)
        return top, idx

    p1, i1 = first_max(pm)
    p2, i2 = first_max([jnp.where(i1 == e, -2.0, pm[e]) for e in range(N_EXPERTS)])
    tot = p1 + p2
    return [jnp.where(i1 == e, p1 / tot, 0.0) + jnp.where(i2 == e, p2 / tot, 0.0) for e in range(N_EXPERTS)]


def _moe_kernel(h_ref, lg_ref, x_ref, g2_ref, wgu_ref, wd_ref, fg_ref, o_ref, acc_ref, gates_ref,
                *, d_expert, final_norm):
    e = pl.program_id(1)

    @pl.when(e == 0)
    def _():
        acc_ref[...] = jnp.zeros_like(acc_ref)
        lane = lax.broadcasted_iota(jnp.int32, gates_ref.shape, 1)
        gates = jnp.zeros(gates_ref.shape, F32)
        for idx, col in enumerate(_route(lg_ref[...])):
            gates = jnp.where(lane == idx, col, gates)
        gates_ref[...] = gates

    lane = lax.broadcasted_iota(jnp.int32, gates_ref.shape, 1)
    gcol = jnp.sum(jnp.where(lane == e, gates_ref[...], 0.0), axis=1, keepdims=True)
    gu = _dot(h_ref[...], wgu_ref[...])
    gt, up = gu[:, :d_expert], gu[:, d_expert:]
    act = gt * _sigmoid(gt) * up
    acc_ref[...] += gcol * _mm1(act, wd_ref[...])

    @pl.when(e == pl.num_programs(1) - 1)
    def _():
        xn = x_ref[...] + g2_ref[...] * acc_ref[...]
        o_ref[...] = _rms(xn) * fg_ref[...] if final_norm else xn


def _moe(h2, logits, x_mid, mods, layer, wgu_bf16, wd_bf16, final_g, lay, tile_off, n_tiles, final_norm):
    n, d = x_mid.shape
    n_exp, _, two_de = wgu_bf16.shape
    tm = TM_TOKEN
    tok = lambda w: pl.BlockSpec((tm, w), lambda i, e: (i, 0))
    assert n == n_tiles * tm
    return pl.pallas_call(
        functools.partial(_moe_kernel, d_expert=two_de // 2, final_norm=final_norm),
        grid=(n_tiles, n_exp),
        in_specs=[tok(d), tok(LANES), tok(d), _mod_spec(lay, tm, layer, 5, d, tile_off, 0),
                  pl.BlockSpec((None, d, two_de), lambda i, e: (e, 0, 0)),
                  pl.BlockSpec((None, two_de // 2, d), lambda i, e: (e, 0, 0)),
                  pl.BlockSpec((1, d), lambda i, e: (0, 0))],
        out_specs=tok(d),
        out_shape=jax.ShapeDtypeStruct((n, d), F32),
        scratch_shapes=[pltpu.VMEM((tm, d), F32), pltpu.VMEM((tm, LANES), F32)],
        compiler_params=_params("parallel", "arbitrary"),
        name="moe",
    )(h2, logits, x_mid, mods, wgu_bf16, wd_bf16, final_g.reshape(1, d))


def kernel(x, c, ctx, c_ctx, ada_w, ada_b, norm1_g, norm2_g, w_in, w_out, gmlp_norm_g, gmlp_ws, gmlp_b,
           hgrn_lb_logits, hgrn_norm_g, rwkv_mu, rwkv_w0, rwkv_w_up, rwkv_a0, rwkv_a_up, rwkv_g_up,
           rwkv_k_k, rwkv_k_a, rwkv_r_k, rwkv_gn_g, rwkv_gn_b, router_w, router_b, moe_w_gate_up,
           moe_w_down, final_norm_g):
    bsz, tl, d = x.shape
    tc = ctx.shape[1]
    depth = ada_w.shape[0]
    lay = _Layout(bsz, tc, tl)
    a_cols = 2 * gmlp_norm_g.shape[1]
    b_cols = 5 * hgrn_norm_g.shape[1]
    c_width = rwkv_a0.shape[1]
    assert lay.n_ctx % TM_MATMUL == 0 and tl % TM_MATMUL == 0 and tc % SCAN_CHUNK == 0

    mod_rows = -(-(bsz + 1) // 8) * 8
    c_rows = jnp.zeros((mod_rows, d), F32).at[:bsz].set(c).at[bsz].set(c_ctx)
    mods = _adaln(c_rows, ada_w, ada_b).reshape(depth, mod_rows, 6, 1, d)

    lbs = jnp.cumsum(jax.nn.softmax(hgrn_lb_logits.astype(F32), axis=1), axis=1)
    lbs = lbs - lbs[:, :1]
    w_in_bf16 = w_in.astype(BF16)
    w_out_bf16 = w_out.astype(BF16)
    wgu_bf16 = moe_w_gate_up.astype(BF16)
    wd_bf16 = moe_w_down.astype(BF16)
    router_w_pad = jnp.zeros((d, LANES), F32).at[:, :N_EXPERTS].set(router_w)
    router_b_pad = jnp.zeros((1, LANES), F32).at[0, :N_EXPERTS].set(router_b)

    x_all = jnp.concatenate([ctx.reshape(lay.n_ctx, d), x.reshape(lay.n_lat, d)], axis=0)
    ctx_tiles = lay.n_ctx // TM_TOKEN
    all_tiles = lay.n // TM_TOKEN

    for l in range(depth):
        last = l == depth - 1
        tile_off = ctx_tiles if last else 0
        n_tiles = all_tiles - tile_off
        w_l = w_in_bf16[l]
        pa = _inproj(x_all, norm1_g[l], mods, l, w_l[:, :a_cols], a_cols, lay)
        pb = _inproj(x_all, norm1_g[l], mods, l, w_l[:, a_cols:a_cols + b_cols], b_cols // 3, lay)
        pc = _inproj(x_all, norm1_g[l], mods, l, w_l[:, a_cols + b_cols:], (w_l.shape[1] - a_cols - b_cols) // 2, lay)

        ya = _gmlp(pa, gmlp_norm_g[l], gmlp_ws[l], gmlp_b[l], tile_off, n_tiles)
        hf = _hgrn(pb, lbs[0, l], lay, reverse=False)
        hb = _hgrn(pb, lbs[1, l], lay, reverse=True)

        w_lora = jnp.zeros((DECAY_LORA + AAA_LORA, 3 * c_width), F32)
        w_lora = w_lora.at[:DECAY_LORA, :c_width].set(rwkv_w_up[l, 0])
        w_lora = w_lora.at[:DECAY_LORA, c_width:2 * c_width].set(rwkv_w_up[l, 1])
        w_lora = w_lora.at[DECAY_LORA:, 2 * c_width:].set(rwkv_a_up[l])
        r, lwf, lwb, k2, v, kk, ab, gate, bonus = _rwkv_prep(
            pc, lay, rwkv_mu[l], w_lora, rwkv_g_up[l], rwkv_w0[l], rwkv_a0[l], rwkv_k_k[l], rwkv_k_a[l],
            rwkv_r_k[l])
        rf = _rwkv_scan(r, lwf, k2, v, kk, ab, lay, reverse=False)
        rb = _rwkv_scan(r, lwb, k2, v, kk, ab, lay, reverse=True)

        x_mid, h2, logits = _outproj(
            x_all, ya, hf, hb, pb, hgrn_norm_g[l], rf, rb, bonus, gate, rwkv_gn_g[l], rwkv_gn_b[l],
            w_out_bf16[l], mods, l, norm2_g[l], router_w_pad, router_b_pad, lay, tile_off, n_tiles)
        x_all = _moe(h2, logits, x_mid, mods, l, wgu_bf16[l], wd_bf16[l], final_norm_g, lay, tile_off,
                     n_tiles, final_norm=last)
    return x_all.reshape(bsz, tl, d)
```

```python
import functools
import math

import jax
import jax.numpy as jnp
from jax import lax
from jax.experimental import pallas as pl
from jax.experimental.pallas import tpu as pltpu

F32 = jnp.float32
BF16 = jnp.bfloat16

GRID_W = 64
A_HEAD_DIM = 128
B_HEAD_DIM = 128
C_HEAD_DIM = 64
GMLP_CHUNK = 128
DECAY_LORA = 64
AAA_LORA = 64
GATE_LORA = 128
N_EXPERTS = 16
N_EXPERT_GROUPS = 4
NORM_EPS = 1e-6
RWKV_GN_EPS = 64e-5
DECAY_SCALE = math.exp(-0.5)

LANES = 128
SCAN_CHUNK = 64
HGRN_SUB = 8
TM_MATMUL = 512
TM_TOKEN = 256
VMEM_LIMIT = 56 * 1024 * 1024

NN = (((1,), (0,)), ((), ()))
NT = (((1,), (1,)), ((), ()))
TN = (((0,), (0,)), ((), ()))


def _dot(a, b, dims=NN):
    return lax.dot_general(a, b, dims, preferred_element_type=F32)


def _mm1(a, b, dims=NN):
    return _dot(a.astype(BF16), b.astype(BF16), dims)


def _split2(x):
    hi = x.astype(BF16)
    lo = (x - hi.astype(F32)).astype(BF16)
    return hi, lo


def _split3(x):
    hi = x.astype(BF16)
    r1 = x - hi.astype(F32)
    mid = r1.astype(BF16)
    lo = (r1 - mid.astype(F32)).astype(BF16)
    return hi, mid, lo


def _mm3(a, b, dims=NN):
    ah, al = _split2(a)
    bh, bl = _split2(b)
    return _dot(ah, bh, dims) + (_dot(al, bh, dims) + _dot(ah, bl, dims))


def _mm_exact_lhs(a_bf16, b, dims=NN):
    h, m, l = _split3(b)
    return _dot(a_bf16, h, dims) + (_dot(a_bf16, m, dims) + _dot(a_bf16, l, dims))


def _mm_exact_rhs(a, b_bf16, dims=NN):
    h, m, l = _split3(a)
    return _dot(h, b_bf16, dims) + (_dot(m, b_bf16, dims) + _dot(l, b_bf16, dims))


def _sigmoid(x):
    return jax.nn.sigmoid(x)


def _rms(x):
    return x * lax.rsqrt(jnp.mean(x * x, axis=-1, keepdims=True) + NORM_EPS)


def _params(*sem):
    return pltpu.CompilerParams(dimension_semantics=sem, vmem_limit_bytes=VMEM_LIMIT)


def _adaln_kernel(c_ref, w_ref, b_ref, o_ref):
    c = c_ref[...]
    o_ref[...] = _mm3(c * _sigmoid(c), w_ref[...]) + b_ref[...]


def _adaln(c_rows, ada_w, ada_b):
    depth, d, n = ada_w.shape
    rows = c_rows.shape[0]
    tn = 1024
    return pl.pallas_call(
        _adaln_kernel,
        grid=(depth, n // tn),
        in_specs=[
            pl.BlockSpec((rows, d), lambda l, j: (0, 0)),
            pl.BlockSpec((None, d, tn), lambda l, j: (l, 0, j)),
            pl.BlockSpec((None, 1, tn), lambda l, j: (l, 0, j)),
        ],
        out_specs=pl.BlockSpec((None, rows, tn), lambda l, j: (l, 0, j)),
        out_shape=jax.ShapeDtypeStruct((depth, rows, n), F32),
        compiler_params=_params("parallel", "parallel"),
        name="adaln",
    )(c_rows, ada_w, ada_b.reshape(depth, 1, n))


class _Layout:
    def __init__(self, bsz, tc, tl):
        self.bsz, self.tc, self.tl = bsz, tc, tl
        self.n_ctx = bsz * tc
        self.n_lat = bsz * tl
        self.n = self.n_ctx + self.n_lat

    def mod_row(self, tile, tm):
        ctx_tiles = self.n_ctx // tm
        return jnp.where(tile < ctx_tiles, self.bsz, (tile - ctx_tiles) // (self.tl // tm))

    def scan_block(self, b, step, reverse):
        ncc = self.tc // SCAN_CHUNK
        ncl = self.tl // SCAN_CHUNK
        if reverse:
            mc = ncc - 1 - step
            ml = ncl - 1 - (step - ncc)
        else:
            mc = step
            ml = step - ncc
        return jnp.where(step < ncc, b * ncc + mc, self.bsz * ncc + b * ncl + ml)


def _mod_spec(lay, tm, layer, part, d, tile_off, grid_pos):
    def index(*ids):
        return (layer, lay.mod_row(ids[grid_pos] + tile_off, tm), part, 0, 0)

    return pl.BlockSpec((None, None, None, 1, d), index)


def _inproj_kernel(x_ref, g_ref, sc_ref, sh_ref, w_ref, o_ref):
    h = _rms(x_ref[...]) * g_ref[...] * (1.0 + sc_ref[...]) + sh_ref[...]
    o_ref[...] = _mm1(h, w_ref[...])


def _inproj(x_all, norm_g, mods, layer, w_bf16, tn, lay):
    n, d = x_all.shape
    cols = w_bf16.shape[1]
    tm = TM_MATMUL
    return pl.pallas_call(
        _inproj_kernel,
        grid=(cols // tn, n // tm),
        in_specs=[
            pl.BlockSpec((tm, d), lambda j, i: (i, 0)),
            pl.BlockSpec((1, d), lambda j, i: (0, 0)),
            _mod_spec(lay, tm, layer, 1, d, 0, 1),
            _mod_spec(lay, tm, layer, 0, d, 0, 1),
            pl.BlockSpec((d, tn), lambda j, i: (0, j)),
        ],
        out_specs=pl.BlockSpec((tm, tn), lambda j, i: (i, j)),
        out_shape=jax.ShapeDtypeStruct((n, cols), F32),
        compiler_params=_params("parallel", "parallel"),
        name="inproj",
    )(x_all, norm_g.reshape(1, d), mods, mods, w_bf16)


def _gmlp_kernel(p_ref, g_ref, ws_ref, wbt_ref, o_ref, *, heads):
    x = p_ref[...]
    ge = 0.5 * x * (1.0 + lax.erf(x * (1.0 / math.sqrt(2.0))))
    width = heads * A_HEAD_DIM
    tm = x.shape[0]
    for h in range(heads):
        lo, hi = h * A_HEAD_DIM, (h + 1) * A_HEAD_DIM
        vn = _rms(ge[:, width + lo:width + hi]) * g_ref[:, lo:hi]
        for c in range(tm // GMLP_CHUNK):
            r0, r1 = c * GMLP_CHUNK, (c + 1) * GMLP_CHUNK
            s = _mm1(ws_ref[h], vn[r0:r1]) + wbt_ref[:, h:h + 1]
            o_ref[r0:r1, lo:hi] = ge[r0:r1, lo:hi] * s


def _gmlp(pa, norm_g, ws, wb, tile_off, n_tiles):
    n, cols = pa.shape
    heads = ws.shape[0]
    width = cols // 2
    tm = TM_TOKEN
    return pl.pallas_call(
        functools.partial(_gmlp_kernel, heads=heads),
        grid=(n_tiles,),
        in_specs=[
            pl.BlockSpec((tm, cols), lambda i: (i + tile_off, 0)),
            pl.BlockSpec((1, width), lambda i: (0, 0)),
            pl.BlockSpec((heads, GMLP_CHUNK, GMLP_CHUNK), lambda i: (0, 0, 0)),
            pl.BlockSpec((GMLP_CHUNK, heads), lambda i: (0, 0)),
        ],
        out_specs=pl.BlockSpec((tm, width), lambda i: (i, 0)),
        out_shape=jax.ShapeDtypeStruct((n_tiles * tm, width), F32),
        compiler_params=_params("parallel"),
        name="gmlp",
    )(pa, norm_g.reshape(1, width), ws.astype(BF16), wb.T)


def _hgrn_kernel(q_ref, i_ref, z_ref, lb_ref, y_ref, st_ref, *, reverse, heads):
    @pl.when(pl.program_id(1) == 0)
    def _():
        st_ref[...] = jnp.zeros_like(st_ref)

    cl, sub, hd = SCAN_CHUNK, HGRN_SUB, B_HEAD_DIM
    q = q_ref[...]
    q = q * _sigmoid(q)
    v = i_ref[...]
    z = z_ref[...]
    lb = lb_ref[...]
    t_lb = jnp.log(lb)
    t_z = jnp.log1p(-lb) + (jnp.minimum(z, 0.0) - jnp.log1p(jnp.exp(-jnp.abs(z))))
    lf = jnp.maximum(t_lb, t_z) + jnp.log1p(jnp.exp(-jnp.abs(t_lb - t_z)))
    k = (1.0 - lb) * _sigmoid(-z)

    row = lax.broadcasted_iota(jnp.int32, (cl, cl), 0)
    col = lax.broadcasted_iota(jnp.int32, (cl, cl), 1)
    ut = (cl - 1 - row) if reverse else row
    us = (cl - 1 - col) if reverse else col

    sel = [us <= ut]
    keeps = []
    half = cl // 2
    while half >= sub:
        mid = (ut // (2 * half)) * (2 * half) + half
        sel.append(us < mid)
        keeps.append((ut >= mid) & (us < mid) & (us >= mid - half))
        half //= 2
    start = (ut // sub) * sub
    sel.append(us < start)
    keeps.append((us >= start) & (us <= ut))
    cums = _mm_exact_lhs(jnp.concatenate(sel, axis=0).astype(BF16), lf)
    c = cums[0:cl]
    total = c[0:1] if reverse else c[cl - 1:cl]
    qe = q * jnp.exp(c)
    ke = k * jnp.exp(total - c)

    scores = [None] * heads
    for lvl, keep in enumerate(keeps):
        ref = cums[(lvl + 1) * cl:(lvl + 2) * cl]
        dq, dk = c - ref, ref - c
        if lvl + 1 < len(keeps):
            dq, dk = jnp.minimum(dq, 0.0), jnp.minimum(dk, 0.0)
        qs = q * jnp.exp(dq)
        ks = k * jnp.exp(dk)
        for h in range(heads):
            lo, hi = h * hd, (h + 1) * hd
            s = jnp.where(keep, _mm1(qs[:, lo:hi], ks[:, lo:hi], NT), 0.0)
            scores[h] = s if scores[h] is None else scores[h] + s

    for h in range(heads):
        lo, hi = h * hd, (h + 1) * hd
        st = st_ref[h]
        y_ref[:, lo:hi] = _mm1(qe[:, lo:hi], st, NT) + _mm1(scores[h], v[:, lo:hi])
        st_ref[h] = st * jnp.exp(total[:, lo:hi]) + _mm1(v[:, lo:hi], ke[:, lo:hi], TN)


def _hgrn(pb, lb_dir, lay, reverse):
    n = pb.shape[0]
    width = pb.shape[1] // 5
    heads = width // B_HEAD_DIM
    steps = (lay.tc + lay.tl) // SCAN_CHUNK
    blk = lambda comp: pl.BlockSpec(
        (SCAN_CHUNK, width), lambda b, s: (lay.scan_block(b, s, reverse), comp))
    return pl.pallas_call(
        functools.partial(_hgrn_kernel, reverse=reverse, heads=heads),
        grid=(lay.bsz, steps),
        in_specs=[blk(0), blk(1), blk(3 if reverse else 2),
                  pl.BlockSpec((1, width), lambda b, s: (0, 0))],
        out_specs=blk(0),
        out_shape=jax.ShapeDtypeStruct((n, width), F32),
        scratch_shapes=[pltpu.VMEM((heads, B_HEAD_DIM, B_HEAD_DIM), F32)],
        compiler_params=_params("parallel", "arbitrary"),
        name="hgrn_bwd" if reverse else "hgrn_fwd",
    )(pb, pb, pb, lb_dir.reshape(1, width))


def _rwkv_prep_kernel(cur_ref, prev_ref, next_ref, mu_ref, wl_ref, gup_ref, w0f_ref, w0b_ref, a0_ref,
                      kk_ref, ka_ref, rk_ref, seg_ref,
                      r_out, lwf_out, lwb_out, k_out, v_out, kk_out, ab_out, g_out, bonus_out,
                      *, ctx_tiles, rows_per_batch, width):
    tile = pl.program_id(0)
    z = cur_ref[...]
    tm, cols = z.shape
    halo = prev_ref.shape[0]
    ext = jnp.concatenate([prev_ref[...], z, next_ref[...]], axis=0)
    left = ext[halo - 1:halo - 1 + tm]
    right = ext[halo + 1:halo + 1 + tm]
    up = ext[0:tm]
    down = ext[2 * halo:2 * halo + tm]

    is_ctx = tile < ctx_tiles
    t = lax.broadcasted_iota(jnp.int32, (tm, 1), 0)
    rowb = t + ((tile - ctx_tiles) * tm) % rows_per_batch
    line = jnp.where(is_ctx, tm, GRID_W)
    pos = t & (line - 1)
    left = jnp.where(pos != 0, left, 0.0)
    right = jnp.where(pos != line - 1, right, 0.0)
    up = jnp.where(rowb >= GRID_W, up, 0.0)
    down = jnp.where(rowb < rows_per_batch - GRID_W, down, 0.0)
    slot = lax.broadcasted_iota(jnp.int32, (1, cols), 1) % 4
    shifted = jnp.where(
        slot == 0, left,
        jnp.where(slot == 1, right,
                  jnp.where(slot == 2, jnp.where(is_ctx, left, up), jnp.where(is_ctx, right, down))))
    xx = z + mu_ref[...] * (shifted - z)

    r = xx[:, 0:width]
    k = xx[:, width:2 * width]
    v = xx[:, 2 * width:3 * width]
    lora = xx[:, 3 * width:3 * width + DECAY_LORA + AAA_LORA]
    lane = lax.broadcasted_iota(jnp.int32, (1, DECAY_LORA + AAA_LORA), 1)
    lora = jnp.where(lane < DECAY_LORA, jnp.tanh(lora), lora)
    up3 = _mm3(lora, wl_ref[...])
    gd = xx[:, 3 * width + DECAY_LORA + AAA_LORA:]
    g_out[...] = _mm3(_sigmoid(gd), gup_ref[...])
    lwf_out[...] = -DECAY_SCALE * _sigmoid(w0f_ref[...] + up3[:, 0:width])
    lwb_out[...] = -DECAY_SCALE * _sigmoid(w0b_ref[...] + up3[:, width:2 * width])
    a = _sigmoid(a0_ref[...] + up3[:, 2 * width:3 * width])

    seg = seg_ref[...]
    kk = k * kk_ref[...]
    kk = kk / jnp.maximum(jnp.sqrt(_mm_exact_rhs(kk * kk, seg)), 1e-12)
    k2 = k * (1.0 + (a - 1.0) * ka_ref[...])
    r_out[...] = r
    k_out[...] = k2
    v_out[...] = v
    kk_out[...] = kk
    ab_out[...] = kk * a
    bonus_out[...] = _mm_exact_rhs(r * k2 * rk_ref[...], seg) * v


def _rwkv_prep(pc, lay, mu, w_lora, g_up, w0, a0, k_k, k_a, r_k):
    n, cols = pc.shape
    width = a0.shape[-1]
    tm = TM_TOKEN
    assert lay.tc == tm and lay.tl % tm == 0 and tm % GRID_W == 0
    halo = GRID_W
    hb = tm // halo
    last = n // halo - 1
    row = lambda a: a.reshape(1, -1)
    hi = lax.broadcasted_iota(jnp.int32, (width, width), 0) // C_HEAD_DIM
    hj = lax.broadcasted_iota(jnp.int32, (width, width), 1) // C_HEAD_DIM
    seg = (hi == hj).astype(BF16)
    full = lambda a: pl.BlockSpec(a.shape, lambda i: (0,) * a.ndim)
    consts = [row(mu), w_lora, g_up, row(w0[0]), row(w0[1]), row(a0), row(k_k), row(k_a), row(r_k), seg]
    out_spec = pl.BlockSpec((tm, width), lambda i: (i, 0))
    return pl.pallas_call(
        functools.partial(_rwkv_prep_kernel, ctx_tiles=lay.n_ctx // tm, rows_per_batch=lay.tl, width=width),
        grid=(n // tm,),
        in_specs=[
            pl.BlockSpec((tm, cols), lambda i: (i, 0)),
            pl.BlockSpec((halo, cols), lambda i: (jnp.maximum(i * hb - 1, 0), 0)),
            pl.BlockSpec((halo, cols), lambda i: (jnp.minimum((i + 1) * hb, last), 0)),
        ] + [full(a) for a in consts],
        out_specs=[out_spec] * 9,
        out_shape=[jax.ShapeDtypeStruct((n, width), F32)] * 9,
        compiler_params=_params("parallel"),
        name="rwkv_prep",
    )(pc, pc, pc, *consts)


def _rwkv_kernel(r_ref, lw_ref, k_ref, v_ref, kk_ref, ab_ref, o_ref, st_ref, *, reverse, pairs):
    @pl.when(pl.program_id(1) == 0)
    def _():
        st_ref[...] = jnp.zeros_like(st_ref)

    cl, hd, pw = SCAN_CHUNK, C_HEAD_DIM, 2 * C_HEAD_DIM
    every = range(pairs)
    cut = lambda x: [x[:, p * pw:(p + 1) * pw] for p in every]
    cat = lambda a, b: jnp.concatenate([a, b], axis=0)

    lw = lw_ref[...]
    row = lax.broadcasted_iota(jnp.int32, (cl, cl), 0)
    col = lax.broadcasted_iota(jnp.int32, (cl, cl), 1)
    tri = (col >= row) if reverse else (col <= row)
    c = _mm_exact_lhs(tri.astype(BF16), lw)
    total = c[0:1] if reverse else c[cl - 1:cl]
    r, k, v, kk, ab = (ref[...] for ref in (r_ref, k_ref, v_ref, kk_ref, ab_ref))
    e_out = jnp.exp(-c)
    e_last = jnp.exp(total - c)
    kkg, rg, kd, bd = cut(kk * jnp.exp(c - lw)), cut(r * jnp.exp(c)), cut(k * e_out), cut(ab * e_out)
    kc, bc, vs = cut(k * e_last), cut(ab * e_last), cut(v)
    decay = cut(jnp.exp(total))
    s0 = [st_ref[p] for p in every]

    lane = lax.broadcasted_iota(jnp.int32, (cl, pw), 1)
    m1 = (lane < hd).astype(F32)
    m2 = 1.0 - m1
    r2 = lax.broadcasted_iota(jnp.int32, (2 * cl, 2 * cl), 0)
    c2 = lax.broadcasted_iota(jnp.int32, (2 * cl, 2 * cl), 1)
    tt, ss = r2 % cl, c2 % cl
    strict = (ss > tt) if reverse else (ss < tt)
    same = (r2 // cl) == (c2 // cl)
    eye = jnp.where(r2 == c2, 1.0, 0.0)
    ti = lax.broadcasted_iota(jnp.int32, (cl, 2 * cl), 0)
    si = lax.broadcasted_iota(jnp.int32, (cl, 2 * cl), 1) % cl
    incl = (si >= ti) if reverse else (si <= ti)

    g1 = [_mm1(cat(kkg[p] * m1, rg[p] * m1), cat(bd[p], kd[p]), NT) for p in every]
    g2 = [_mm1(cat(kkg[p] * m2, rg[p] * m2), cat(kd[p], bd[p]), NT) for p in every]
    xs = [_mm1(cat(kkg[p], rg[p]), s0[p], NT) for p in every]
    gl = [jnp.where(strict, cat(g1[p][0:cl], g2[p][0:cl]), 0.0) for p in every]
    l_diag = [jnp.where(same, gl[p], 0.0) for p in every]
    l_anti = [jnp.where(same, 0.0, gl[p]) for p in every]
    lkv = [_mm1(l_anti[p], cat(vs[p] * m2, vs[p] * m1)) for p in every]

    inv = [eye - l_diag[p] for p in every]
    pwr = [_mm1(l_diag[p], l_diag[p]) for p in every]
    n_sq = int(math.log2(cl)) - 1
    for i in range(n_sq):
        inv = [inv[p] + _mm1(inv[p], pwr[p]) for p in every]
        if i + 1 < n_sq:
            pwr = [_mm1(pwr[p], pwr[p]) for p in every]

    rhs = [xs[p][0:cl] + lkv[p][0:cl] + lkv[p][cl:] for p in every]
    ust = [_mm1(inv[p], cat(rhs[p] * m1, rhs[p] * m2)) for p in every]
    u = [ust[p][0:cl] + ust[p][cl:] for p in every]
    o1 = [_mm1(jnp.where(incl, g1[p][cl:], 0.0), cat(-u[p] * m1, vs[p] * m1)) for p in every]
    o2 = [_mm1(jnp.where(incl, g2[p][cl:], 0.0), cat(vs[p] * m2, -u[p] * m2)) for p in every]
    upd = [_mm1(cat(vs[p], -u[p]), cat(kc[p], bc[p]), TN) for p in every]
    for p in every:
        o_ref[:, p * pw:(p + 1) * pw] = xs[p][cl:] + o1[p] + o2[p]
        st_ref[p] = s0[p] * decay[p] + jnp.where(same, upd[p], 0.0)


def _rwkv_scan(r, lw, k, v, kk, ab, lay, reverse):
    n, width = r.shape
    pairs = width // (2 * C_HEAD_DIM)
    steps = (lay.tc + lay.tl) // SCAN_CHUNK
    blk = pl.BlockSpec((SCAN_CHUNK, width), lambda b, s: (lay.scan_block(b, s, reverse), 0))
    return pl.pallas_call(
        functools.partial(_rwkv_kernel, reverse=reverse, pairs=pairs),
        grid=(lay.bsz, steps),
        in_specs=[blk] * 6,
        out_specs=blk,
        out_shape=jax.ShapeDtypeStruct((n, width), F32),
        scratch_shapes=[pltpu.VMEM((pairs, 2 * C_HEAD_DIM, 2 * C_HEAD_DIM), F32)],
        compiler_params=_params("parallel", "arbitrary"),
        name="rwkv_bwd" if reverse else "rwkv_fwd",
    )(r, lw, k, v, kk, ab)


def _outproj_kernel(x_ref, ya_ref, hf_ref, hb_ref, og_ref, hg_ref, rf_ref, rb_ref, bonus_ref, gate_ref,
                    gng_ref, gnb_ref, avg_ref, w_ref, g1_ref, n2_ref, sc2_ref, sh2_ref, rw_ref, rbias_ref,
                    xo_ref, h2_ref, lg_ref, *, a_width, b_width):
    o = hf_ref[...] + hb_ref[...]
    og = og_ref[...]
    yb = []
    for h in range(b_width // B_HEAD_DIM):
        lo, hi = h * B_HEAD_DIM, (h + 1) * B_HEAD_DIM
        yb.append(_rms(o[:, lo:hi]) * hg_ref[:, lo:hi] * _sigmoid(og[:, lo:hi]))
    yb = jnp.concatenate(yb, axis=1)

    rw = rf_ref[...] + rb_ref[...]
    avg = avg_ref[...]
    xc = rw - _mm_exact_rhs(rw, avg)
    var = _mm_exact_rhs(xc * xc, avg)
    yc = (xc * lax.rsqrt(var + RWKV_GN_EPS) * gng_ref[...] + gnb_ref[...] + bonus_ref[...]) * gate_ref[...]

    y = (_mm1(ya_ref[...], w_ref[0:a_width]) + _mm1(yb, w_ref[a_width:a_width + b_width])
         + _mm1(yc, w_ref[a_width + b_width:]))
    xn = x_ref[...] + g1_ref[...] * y
    xo_ref[...] = xn
    h2 = _rms(xn) * n2_ref[...] * (1.0 + sc2_ref[...]) + sh2_ref[...]
    h2_ref[...] = h2.astype(BF16)
    lg_ref[...] = _mm3(h2, rw_ref[...]) + rbias_ref[...]


def _outproj(x_all, ya, hf, hb, pb, hgrn_g, rf, rb, bonus, gate, gn_g, gn_b, w_out_bf16, mods, layer,
             norm2_g, router_w_pad, router_b_pad, lay, tile_off, n_tiles):
    n, d = x_all.shape
    a_width = ya.shape[1]
    b_width = hf.shape[1]
    c_width = rf.shape[1]
    tm = TM_TOKEN
    hi = lax.broadcasted_iota(jnp.int32, (c_width, c_width), 0) // C_HEAD_DIM
    hj = lax.broadcasted_iota(jnp.int32, (c_width, c_width), 1) // C_HEAD_DIM
    avg = jnp.where(hi == hj, 1.0 / C_HEAD_DIM, 0.0).astype(BF16)
    tok = lambda w, comp=0, off=tile_off: pl.BlockSpec((tm, w), lambda i: (i + off, comp))
    full = lambda a: pl.BlockSpec(a.shape, lambda i: (0,) * a.ndim)
    row = lambda a: a.reshape(1, -1)
    mod = lambda part: _mod_spec(lay, tm, layer, part, d, tile_off, 0)
    hg, gg, gb, n2 = row(hgrn_g), row(gn_g), row(gn_b), row(norm2_g)
    rows = n_tiles * tm
    return pl.pallas_call(
        functools.partial(_outproj_kernel, a_width=a_width, b_width=b_width),
        grid=(n_tiles,),
        in_specs=[tok(d), tok(a_width, off=0), tok(b_width), tok(b_width), tok(b_width, 4), full(hg),
                  tok(c_width), tok(c_width), tok(c_width), tok(c_width), full(gg), full(gb), full(avg),
                  full(w_out_bf16), mod(2), full(n2), mod(4), mod(3), full(router_w_pad), full(router_b_pad)],
        out_specs=[tok(d, off=0), tok(d, off=0), tok(LANES, off=0)],
        out_shape=[jax.ShapeDtypeStruct((rows, d), F32), jax.ShapeDtypeStruct((rows, d), BF16),
                   jax.ShapeDtypeStruct((rows, LANES), F32)],
        compiler_params=_params("parallel"),
        name="outproj",
    )(x_all, ya, hf, hb, pb, hg, rf, rb, bonus, gate, gg, gb, avg, w_out_bf16, mods, n2, mods, mods,
      router_w_pad, router_b_pad)


def _route(logits):
    per_group = N_EXPERTS // N_EXPERT_GROUPS
    le = [logits[:, e:e + 1] for e in range(N_EXPERTS)]
    mx = functools.reduce(jnp.maximum, le)
    ex = [jnp.exp(l - mx) for l in le]
    den = functools.reduce(jnp.add, ex)
    p = [e / den for e in ex]

    def top2_sum(vals):
        a, b, c, d = vals
        h1, l1, h2, l2 = jnp.maximum(a, b), jnp.minimum(a, b), jnp.maximum(c, d), jnp.minimum(c, d)
        return jnp.maximum(h1, h2) + jnp.maximum(jnp.minimum(h1, h2), jnp.maximum(l1, l2))

    assert per_group == 4
    score = [top2_sum(p[g * per_group:(g + 1) * per_group]) for g in range(N_EXPERT_GROUPS)]
    best, sel = score[0], jnp.zeros_like(score[0], dtype=jnp.int32)
    for g in range(1, N_EXPERT_GROUPS):
        better = score[g] > best
        best = jnp.where(better, score[g], best)
        sel = jnp.where(better, g, sel)
    pm = [jnp.where(sel == e // per_group, p[e], -1.0) for e in range(N_EXPERTS)]

    def first_max(vals):
        top = functools.reduce(jnp.maximum, vals)
        idx = jnp.full_like(sel, N_EXPERTS)
        for e in reversed(range(N_EXPERTS)):
            idx = jnp.where(vals[e] == top, e, idx)
        return top, idx

    p1, i1 = first_max(pm)
    p2, i2 = first_max([jnp.where(i1 == e, -2.0, pm[e]) for e in range(N_EXPERTS)])
    tot = p1 + p2
    return [jnp.where(i1 == e, p1 / tot, 0.0) + jnp.where(i2 == e, p2 / tot, 0.0) for e in range(N_EXPERTS)]


def _moe_kernel(h_ref, lg_ref, x_ref, g2_ref, wgu_ref, wd_ref, fg_ref, o_ref, acc_ref, gates_ref,
                *, d_expert, final_norm):
    e = pl.program_id(1)

    @pl.when(e == 0)
    def _():
        acc_ref[...] = jnp.zeros_like(acc_ref)
        lane = lax.broadcasted_iota(jnp.int32, gates_ref.shape, 1)
        gates = jnp.zeros(gates_ref.shape, F32)
        for idx, col in enumerate(_route(lg_ref[...])):
            gates = jnp.where(lane == idx, col, gates)
        gates_ref[...] = gates

    lane = lax.broadcasted_iota(jnp.int32, gates_ref.shape, 1)
    gcol = jnp.sum(jnp.where(lane == e, gates_ref[...], 0.0), axis=1, keepdims=True)
    gu = _dot(h_ref[...], wgu_ref[...])
    gt, up = gu[:, :d_expert], gu[:, d_expert:]
    act = gt * _sigmoid(gt) * up
    acc_ref[...] += gcol * _mm1(act, wd_ref[...])

    @pl.when(e == pl.num_programs(1) - 1)
    def _():
        xn = x_ref[...] + g2_ref[...] * acc_ref[...]
        o_ref[...] = _rms(xn) * fg_ref[...] if final_norm else xn


def _moe(h2, logits, x_mid, mods, layer, wgu_bf16, wd_bf16, final_g, lay, tile_off, n_tiles, final_norm):
    n, d = x_mid.shape
    n_exp, _, two_de = wgu_bf16.shape
    tm = TM_TOKEN
    tok = lambda w: pl.BlockSpec((tm, w), lambda i, e: (i, 0))
    assert n == n_tiles * tm
    return pl.pallas_call(
        functools.partial(_moe_kernel, d_expert=two_de // 2, final_norm=final_norm),
        grid=(n_tiles, n_exp),
        in_specs=[tok(d), tok(LANES), tok(d), _mod_spec(lay, tm, layer, 5, d, tile_off, 0),
                  pl.BlockSpec((None, d, two_de), lambda i, e: (e, 0, 0)),
                  pl.BlockSpec((None, two_de // 2, d), lambda i, e: (e, 0, 0)),
                  pl.BlockSpec((1, d), lambda i, e: (0, 0))],
        out_specs=tok(d),
        out_shape=jax.ShapeDtypeStruct((n, d), F32),
        scratch_shapes=[pltpu.VMEM((tm, d), F32), pltpu.VMEM((tm, LANES), F32)],
        compiler_params=_params("parallel", "arbitrary"),
        name="moe",
    )(h2, logits, x_mid, mods, wgu_bf16, wd_bf16, final_g.reshape(1, d))


def kernel(x, c, ctx, c_ctx, ada_w, ada_b, norm1_g, norm2_g, w_in, w_out, gmlp_norm_g, gmlp_ws, gmlp_b,
           hgrn_lb_logits, hgrn_norm_g, rwkv_mu, rwkv_w0, rwkv_w_up, rwkv_a0, rwkv_a_up, rwkv_g_up,
           rwkv_k_k, rwkv_k_a, rwkv_r_k, rwkv_gn_g, rwkv_gn_b, router_w, router_b, moe_w_gate_up,
           moe_w_down, final_norm_g):
    bsz, tl, d = x.shape
    tc = ctx.shape[1]
    depth = ada_w.shape[0]
    lay = _Layout(bsz, tc, tl)
    a_cols = 2 * gmlp_norm_g.shape[1]
    b_cols = 5 * hgrn_norm_g.shape[1]
    c_width = rwkv_a0.shape[1]
    assert lay.n_ctx % TM_MATMUL == 0 and tl % TM_MATMUL == 0 and tc % SCAN_CHUNK == 0

    mod_rows = -(-(bsz + 1) // 8) * 8
    c_rows = jnp.zeros((mod_rows, d), F32).at[:bsz].set(c).at[bsz].set(c_ctx)
    mods = _adaln(c_rows, ada_w, ada_b).reshape(depth, mod_rows, 6, 1, d)

    lbs = jnp.cumsum(jax.nn.softmax(hgrn_lb_logits.astype(F32), axis=1), axis=1)
    lbs = lbs - lbs[:, :1]
    w_in_bf16 = w_in.astype(BF16)
    w_out_bf16 = w_out.astype(BF16)
    wgu_bf16 = moe_w_gate_up.astype(BF16)
    wd_bf16 = moe_w_down.astype(BF16)
    router_w_pad = jnp.zeros((d, LANES), F32).at[:, :N_EXPERTS].set(router_w)
    router_b_pad = jnp.zeros((1, LANES), F32).at[0, :N_EXPERTS].set(router_b)

    x_all = jnp.concatenate([ctx.reshape(lay.n_ctx, d), x.reshape(lay.n_lat, d)], axis=0)
    ctx_tiles = lay.n_ctx // TM_TOKEN
    all_tiles = lay.n // TM_TOKEN

    for l in range(depth):
        last = l == depth - 1
        tile_off = ctx_tiles if last else 0
        n_tiles = all_tiles - tile_off
        w_l = w_in_bf16[l]
        pa = _inproj(x_all, norm1_g[l], mods, l, w_l[:, :a_cols], a_cols, lay)
        pb = _inproj(x_all, norm1_g[l], mods, l, w_l[:, a_cols:a_cols + b_cols], b_cols // 3, lay)
        pc = _inproj(x_all, norm1_g[l], mods, l, w_l[:, a_cols + b_cols:], (w_l.shape[1] - a_cols - b_cols) // 2, lay)

        ya = _gmlp(pa, gmlp_norm_g[l], gmlp_ws[l], gmlp_b[l], tile_off, n_tiles)
        hf = _hgrn(pb, lbs[0, l], lay, reverse=False)
        hb = _hgrn(pb, lbs[1, l], lay, reverse=True)

        w_lora = jnp.zeros((DECAY_LORA + AAA_LORA, 3 * c_width), F32)
        w_lora = w_lora.at[:DECAY_LORA, :c_width].set(rwkv_w_up[l, 0])
        w_lora = w_lora.at[:DECAY_LORA, c_width:2 * c_width].set(rwkv_w_up[l, 1])
        w_lora = w_lora.at[DECAY_LORA:, 2 * c_width:].set(rwkv_a_up[l])
        r, lwf, lwb, k2, v, kk, ab, gate, bonus = _rwkv_prep(
            pc, lay, rwkv_mu[l], w_lora, rwkv_g_up[l], rwkv_w0[l], rwkv_a0[l], rwkv_k_k[l], rwkv_k_a[l],
            rwkv_r_k[l])
        rf = _rwkv_scan(r, lwf, k2, v, kk, ab, lay, reverse=False)
        rb = _rwkv_scan(r, lwb, k2, v, kk, ab, lay, reverse=True)

        x_mid, h2, logits = _outproj(
            x_all, ya, hf, hb, pb, hgrn_norm_g[l], rf, rb, bonus, gate, rwkv_gn_g[l], rwkv_gn_b[l],
            w_out_bf16[l], mods, l, norm2_g[l], router_w_pad, router_b_pad, lay, tile_off, n_tiles)
        x_all = _moe(h2, logits, x_mid, mods, l, wgu_bf16[l], wd_bf16[l], final_norm_g, lay, tile_off,
                     n_tiles, final_norm=last)
    return x_all.reshape(bsz, tl, d)
```

```python
import functools
import math

import jax
import jax.numpy as jnp
from jax import lax
from jax.experimental import pallas as pl
from jax.experimental.pallas import tpu as pltpu

F32 = jnp.float32
BF16 = jnp.bfloat16

GRID_W = 64
A_HEAD_DIM = 128
B_HEAD_DIM = 128
C_HEAD_DIM = 64
GMLP_CHUNK = 128
DECAY_LORA = 64
AAA_LORA = 64
GATE_LORA = 128
N_EXPERTS = 16
N_EXPERT_GROUPS = 4
NORM_EPS = 1e-6
RWKV_GN_EPS = 64e-5
DECAY_SCALE = math.exp(-0.5)

LANES = 128
SCAN_CHUNK = 64
HGRN_SUB = 8
TM_MATMUL = 512
TM_TOKEN = 256
MOE_TM = 1024
MOE_ROWS = 160
VMEM_LIMIT = 56 * 1024 * 1024

NN = (((1,), (0,)), ((), ()))
NT = (((1,), (1,)), ((), ()))
TN = (((0,), (0,)), ((), ()))


def _dot(a, b, dims=NN):
    return lax.dot_general(a, b, dims, preferred_element_type=F32)


def _mm1(a, b, dims=NN):
    return _dot(a.astype(BF16), b.astype(BF16), dims)


def _split2(x):
    hi = x.astype(BF16)
    lo = (x - hi.astype(F32)).astype(BF16)
    return hi, lo


def _split3(x):
    hi = x.astype(BF16)
    r1 = x - hi.astype(F32)
    mid = r1.astype(BF16)
    lo = (r1 - mid.astype(F32)).astype(BF16)
    return hi, mid, lo


def _mm3(a, b, dims=NN):
    ah, al = _split2(a)
    bh, bl = _split2(b)
    return _dot(ah, bh, dims) + (_dot(al, bh, dims) + _dot(ah, bl, dims))


def _mm_exact_lhs(a_bf16, b, dims=NN):
    h, m, l = _split3(b)
    return _dot(a_bf16, h, dims) + (_dot(a_bf16, m, dims) + _dot(a_bf16, l, dims))


def _mm_exact_rhs(a, b_bf16, dims=NN):
    h, m, l = _split3(a)
    return _dot(h, b_bf16, dims) + (_dot(m, b_bf16, dims) + _dot(l, b_bf16, dims))


def _sigmoid(x):
    return jax.nn.sigmoid(x)


def _rms(x):
    return x * lax.rsqrt(jnp.mean(x * x, axis=-1, keepdims=True) + NORM_EPS)


def _params(*sem):
    return pltpu.CompilerParams(dimension_semantics=sem, vmem_limit_bytes=VMEM_LIMIT)


def _adaln_kernel(c_ref, w_ref, b_ref, o_ref):
    c = c_ref[...]
    o_ref[...] = _mm3(c * _sigmoid(c), w_ref[...]) + b_ref[...]


def _adaln(c_rows, ada_w, ada_b):
    depth, d, n = ada_w.shape
    rows = c_rows.shape[0]
    tn = 1024
    return pl.pallas_call(
        _adaln_kernel,
        grid=(depth, n // tn),
        in_specs=[
            pl.BlockSpec((rows, d), lambda l, j: (0, 0)),
            pl.BlockSpec((None, d, tn), lambda l, j: (l, 0, j)),
            pl.BlockSpec((None, 1, tn), lambda l, j: (l, 0, j)),
        ],
        out_specs=pl.BlockSpec((None, rows, tn), lambda l, j: (l, 0, j)),
        out_shape=jax.ShapeDtypeStruct((depth, rows, n), F32),
        compiler_params=_params("parallel", "parallel"),
        name="adaln",
    )(c_rows, ada_w, ada_b.reshape(depth, 1, n))


class _Layout:
    def __init__(self, bsz, tc, tl):
        self.bsz, self.tc, self.tl = bsz, tc, tl
        self.n_ctx = bsz * tc
        self.n_lat = bsz * tl
        self.n = self.n_ctx + self.n_lat

    def mod_row(self, tile, tm):
        ctx_tiles = self.n_ctx // tm
        return jnp.where(tile < ctx_tiles, self.bsz, (tile - ctx_tiles) // (self.tl // tm))

    def scan_block(self, b, step, reverse):
        ncc = self.tc // SCAN_CHUNK
        ncl = self.tl // SCAN_CHUNK
        if reverse:
            mc = ncc - 1 - step
            ml = ncl - 1 - (step - ncc)
        else:
            mc = step
            ml = step - ncc
        return jnp.where(step < ncc, b * ncc + mc, self.bsz * ncc + b * ncl + ml)


def _mod_spec(lay, tm, layer, part, d, tile_off, grid_pos):
    def index(*ids):
        return (layer, lay.mod_row(ids[grid_pos] + tile_off, tm), part, 0, 0)

    return pl.BlockSpec((None, None, None, 1, d), index)


def _inproj_kernel(x_ref, g_ref, sc_ref, sh_ref, w_ref, o_ref):
    h = _rms(x_ref[...]) * g_ref[...] * (1.0 + sc_ref[...]) + sh_ref[...]
    o_ref[...] = _mm1(h, w_ref[...])


def _inproj(x_all, norm_g, mods, layer, w_bf16, tn, lay):
    n, d = x_all.shape
    cols = w_bf16.shape[1]
    tm = TM_MATMUL
    return pl.pallas_call(
        _inproj_kernel,
        grid=(cols // tn, n // tm),
        in_specs=[
            pl.BlockSpec((tm, d), lambda j, i: (i, 0)),
            pl.BlockSpec((1, d), lambda j, i: (0, 0)),
            _mod_spec(lay, tm, layer, 1, d, 0, 1),
            _mod_spec(lay, tm, layer, 0, d, 0, 1),
            pl.BlockSpec((d, tn), lambda j, i: (0, j)),
        ],
        out_specs=pl.BlockSpec((tm, tn), lambda j, i: (i, j)),
        out_shape=jax.ShapeDtypeStruct((n, cols), F32),
        compiler_params=_params("parallel", "parallel"),
        name="inproj",
    )(x_all, norm_g.reshape(1, d), mods, mods, w_bf16)


def _gmlp_kernel(p_ref, g_ref, ws_ref, wbt_ref, o_ref, *, heads):
    x = p_ref[...]
    ge = 0.5 * x * (1.0 + lax.erf(x * (1.0 / math.sqrt(2.0))))
    width = heads * A_HEAD_DIM
    tm = x.shape[0]
    for h in range(heads):
        lo, hi = h * A_HEAD_DIM, (h + 1) * A_HEAD_DIM
        vn = _rms(ge[:, width + lo:width + hi]) * g_ref[:, lo:hi]
        for c in range(tm // GMLP_CHUNK):
            r0, r1 = c * GMLP_CHUNK, (c + 1) * GMLP_CHUNK
            s = _mm1(ws_ref[h], vn[r0:r1]) + wbt_ref[:, h:h + 1]
            o_ref[r0:r1, lo:hi] = ge[r0:r1, lo:hi] * s


def _gmlp(pa, norm_g, ws, wb, tile_off, n_tiles):
    n, cols = pa.shape
    heads = ws.shape[0]
    width = cols // 2
    tm = TM_TOKEN
    return pl.pallas_call(
        functools.partial(_gmlp_kernel, heads=heads),
        grid=(n_tiles,),
        in_specs=[
            pl.BlockSpec((tm, cols), lambda i: (i + tile_off, 0)),
            pl.BlockSpec((1, width), lambda i: (0, 0)),
            pl.BlockSpec((heads, GMLP_CHUNK, GMLP_CHUNK), lambda i: (0, 0, 0)),
            pl.BlockSpec((GMLP_CHUNK, heads), lambda i: (0, 0)),
        ],
        out_specs=pl.BlockSpec((tm, width), lambda i: (i, 0)),
        out_shape=jax.ShapeDtypeStruct((n_tiles * tm, width), F32),
        compiler_params=_params("parallel"),
        name="gmlp",
    )(pa, norm_g.reshape(1, width), ws.astype(BF16), wb.T)


def _hgrn_kernel(q_ref, i_ref, z_ref, lb_ref, y_ref, st_ref, *, reverse, heads):
    @pl.when(pl.program_id(1) == 0)
    def _():
        st_ref[...] = jnp.zeros_like(st_ref)

    cl, sub, hd = SCAN_CHUNK, HGRN_SUB, B_HEAD_DIM
    q = q_ref[...]
    q = q * _sigmoid(q)
    v = i_ref[...]
    z = z_ref[...]
    lb = lb_ref[...]
    t_lb = jnp.log(lb)
    t_z = jnp.log1p(-lb) + (jnp.minimum(z, 0.0) - jnp.log1p(jnp.exp(-jnp.abs(z))))
    lf = jnp.maximum(t_lb, t_z) + jnp.log1p(jnp.exp(-jnp.abs(t_lb - t_z)))
    k = (1.0 - lb) * _sigmoid(-z)

    row = lax.broadcasted_iota(jnp.int32, (cl, cl), 0)
    col = lax.broadcasted_iota(jnp.int32, (cl, cl), 1)
    ut = (cl - 1 - row) if reverse else row
    us = (cl - 1 - col) if reverse else col

    sel = [us <= ut]
    keeps = []
    half = cl // 2
    while half >= sub:
        mid = (ut // (2 * half)) * (2 * half) + half
        sel.append(us < mid)
        keeps.append((ut >= mid) & (us < mid) & (us >= mid - half))
        half //= 2
    start = (ut // sub) * sub
    sel.append(us < start)
    keeps.append((us >= start) & (us <= ut))
    cums = _mm_exact_lhs(jnp.concatenate(sel, axis=0).astype(BF16), lf)
    c = cums[0:cl]
    total = c[0:1] if reverse else c[cl - 1:cl]
    qe = q * jnp.exp(c)
    ke = k * jnp.exp(total - c)

    scores = [None] * heads
    for lvl, keep in enumerate(keeps):
        ref = cums[(lvl + 1) * cl:(lvl + 2) * cl]
        dq, dk = c - ref, ref - c
        if lvl + 1 < len(keeps):
            dq, dk = jnp.minimum(dq, 0.0), jnp.minimum(dk, 0.0)
        qs = q * jnp.exp(dq)
        ks = k * jnp.exp(dk)
        for h in range(heads):
            lo, hi = h * hd, (h + 1) * hd
            s = jnp.where(keep, _mm1(qs[:, lo:hi], ks[:, lo:hi], NT), 0.0)
            scores[h] = s if scores[h] is None else scores[h] + s

    for h in range(heads):
        lo, hi = h * hd, (h + 1) * hd
        st = st_ref[h]
        y_ref[:, lo:hi] = _mm1(qe[:, lo:hi], st, NT) + _mm1(scores[h], v[:, lo:hi])
        st_ref[h] = st * jnp.exp(total[:, lo:hi]) + _mm1(v[:, lo:hi], ke[:, lo:hi], TN)


def _hgrn(pb, lb_dir, lay, reverse):
    n = pb.shape[0]
    width = pb.shape[1] // 5
    heads = width // B_HEAD_DIM
    steps = (lay.tc + lay.tl) // SCAN_CHUNK
    blk = lambda comp: pl.BlockSpec(
        (SCAN_CHUNK, width), lambda b, s: (lay.scan_block(b, s, reverse), comp))
    return pl.pallas_call(
        functools.partial(_hgrn_kernel, reverse=reverse, heads=heads),
        grid=(lay.bsz, steps),
        in_specs=[blk(0), blk(1), blk(3 if reverse else 2),
                  pl.BlockSpec((1, width), lambda b, s: (0, 0))],
        out_specs=blk(0),
        out_shape=jax.ShapeDtypeStruct((n, width), F32),
        scratch_shapes=[pltpu.VMEM((heads, B_HEAD_DIM, B_HEAD_DIM), F32)],
        compiler_params=_params("parallel", "arbitrary"),
        name="hgrn_bwd" if reverse else "hgrn_fwd",
    )(pb, pb, pb, lb_dir.reshape(1, width))


def _rwkv_prep_kernel(cur_ref, prev_ref, next_ref, mu_ref, wl_ref, gup_ref, w0f_ref, w0b_ref, a0_ref,
                      kk_ref, ka_ref, rk_ref, seg_ref,
                      r_out, lwf_out, lwb_out, k_out, v_out, kk_out, ab_out, g_out, bonus_out,
                      *, ctx_tiles, rows_per_batch, width):
    tile = pl.program_id(0)
    z = cur_ref[...]
    tm, cols = z.shape
    halo = prev_ref.shape[0]
    ext = jnp.concatenate([prev_ref[...], z, next_ref[...]], axis=0)
    left = ext[halo - 1:halo - 1 + tm]
    right = ext[halo + 1:halo + 1 + tm]
    up = ext[0:tm]
    down = ext[2 * halo:2 * halo + tm]

    is_ctx = tile < ctx_tiles
    t = lax.broadcasted_iota(jnp.int32, (tm, 1), 0)
    rowb = t + ((tile - ctx_tiles) * tm) % rows_per_batch
    line = jnp.where(is_ctx, tm, GRID_W)
    pos = t & (line - 1)
    left = jnp.where(pos != 0, left, 0.0)
    right = jnp.where(pos != line - 1, right, 0.0)
    up = jnp.where(rowb >= GRID_W, up, 0.0)
    down = jnp.where(rowb < rows_per_batch - GRID_W, down, 0.0)
    slot = lax.broadcasted_iota(jnp.int32, (1, cols), 1) % 4
    shifted = jnp.where(
        slot == 0, left,
        jnp.where(slot == 1, right,
                  jnp.where(slot == 2, jnp.where(is_ctx, left, up), jnp.where(is_ctx, right, down))))
    xx = z + mu_ref[...] * (shifted - z)

    r = xx[:, 0:width]
    k = xx[:, width:2 * width]
    v = xx[:, 2 * width:3 * width]
    lora = xx[:, 3 * width:3 * width + DECAY_LORA + AAA_LORA]
    lane = lax.broadcasted_iota(jnp.int32, (1, DECAY_LORA + AAA_LORA), 1)
    lora = jnp.where(lane < DECAY_LORA, jnp.tanh(lora), lora)
    up3 = _mm3(lora, wl_ref[...])
    gd = xx[:, 3 * width + DECAY_LORA + AAA_LORA:]
    g_out[...] = _mm3(_sigmoid(gd), gup_ref[...])
    lwf_out[...] = -DECAY_SCALE * _sigmoid(w0f_ref[...] + up3[:, 0:width])
    lwb_out[...] = -DECAY_SCALE * _sigmoid(w0b_ref[...] + up3[:, width:2 * width])
    a = _sigmoid(a0_ref[...] + up3[:, 2 * width:3 * width])

    seg = seg_ref[...]
    kk = k * kk_ref[...]
    kk = kk / jnp.maximum(jnp.sqrt(_mm_exact_rhs(kk * kk, seg)), 1e-12)
    k2 = k * (1.0 + (a - 1.0) * ka_ref[...])
    r_out[...] = r
    k_out[...] = k2
    v_out[...] = v
    kk_out[...] = kk
    ab_out[...] = kk * a
    bonus_out[...] = _mm_exact_rhs(r * k2 * rk_ref[...], seg) * v


def _rwkv_prep(pc, lay, mu, w_lora, g_up, w0, a0, k_k, k_a, r_k):
    n, cols = pc.shape
    width = a0.shape[-1]
    tm = TM_TOKEN
    assert lay.tc == tm and lay.tl % tm == 0 and tm % GRID_W == 0
    halo = GRID_W
    hb = tm // halo
    last = n // halo - 1
    row = lambda a: a.reshape(1, -1)
    hi = lax.broadcasted_iota(jnp.int32, (width, width), 0) // C_HEAD_DIM
    hj = lax.broadcasted_iota(jnp.int32, (width, width), 1) // C_HEAD_DIM
    seg = (hi == hj).astype(BF16)
    full = lambda a: pl.BlockSpec(a.shape, lambda i: (0,) * a.ndim)
    consts = [row(mu), w_lora, g_up, row(w0[0]), row(w0[1]), row(a0), row(k_k), row(k_a), row(r_k), seg]
    out_spec = pl.BlockSpec((tm, width), lambda i: (i, 0))
    return pl.pallas_call(
        functools.partial(_rwkv_prep_kernel, ctx_tiles=lay.n_ctx // tm, rows_per_batch=lay.tl, width=width),
        grid=(n // tm,),
        in_specs=[
            pl.BlockSpec((tm, cols), lambda i: (i, 0)),
            pl.BlockSpec((halo, cols), lambda i: (jnp.maximum(i * hb - 1, 0), 0)),
            pl.BlockSpec((halo, cols), lambda i: (jnp.minimum((i + 1) * hb, last), 0)),
        ] + [full(a) for a in consts],
        out_specs=[out_spec] * 9,
        out_shape=[jax.ShapeDtypeStruct((n, width), F32)] * 9,
        compiler_params=_params("parallel"),
        name="rwkv_prep",
    )(pc, pc, pc, *consts)


def _rwkv_kernel(r_ref, lw_ref, k_ref, v_ref, kk_ref, ab_ref, o_ref, st_ref, *, reverse, pairs):
    @pl.when(pl.program_id(1) == 0)
    def _():
        st_ref[...] = jnp.zeros_like(st_ref)

    cl, hd, pw = SCAN_CHUNK, C_HEAD_DIM, 2 * C_HEAD_DIM
    every = range(pairs)
    cut = lambda x: [x[:, p * pw:(p + 1) * pw] for p in every]
    cat = lambda a, b: jnp.concatenate([a, b], axis=0)

    lw = lw_ref[...]
    row = lax.broadcasted_iota(jnp.int32, (cl, cl), 0)
    col = lax.broadcasted_iota(jnp.int32, (cl, cl), 1)
    tri = (col >= row) if reverse else (col <= row)
    c = _mm_exact_lhs(tri.astype(BF16), lw)
    total = c[0:1] if reverse else c[cl - 1:cl]
    r, k, v, kk, ab = (ref[...] for ref in (r_ref, k_ref, v_ref, kk_ref, ab_ref))
    e_out = jnp.exp(-c)
    e_last = jnp.exp(total - c)
    kkg, rg, kd, bd = cut(kk * jnp.exp(c - lw)), cut(r * jnp.exp(c)), cut(k * e_out), cut(ab * e_out)
    kc, bc, vs = cut(k * e_last), cut(ab * e_last), cut(v)
    decay = cut(jnp.exp(total))
    s0 = [st_ref[p] for p in every]

    lane = lax.broadcasted_iota(jnp.int32, (cl, pw), 1)
    m1 = (lane < hd).astype(F32)
    m2 = 1.0 - m1
    r2 = lax.broadcasted_iota(jnp.int32, (2 * cl, 2 * cl), 0)
    c2 = lax.broadcasted_iota(jnp.int32, (2 * cl, 2 * cl), 1)
    tt, ss = r2 % cl, c2 % cl
    strict = (ss > tt) if reverse else (ss < tt)
    same = (r2 // cl) == (c2 // cl)
    eye = jnp.where(r2 == c2, 1.0, 0.0)
    ti = lax.broadcasted_iota(jnp.int32, (cl, 2 * cl), 0)
    si = lax.broadcasted_iota(jnp.int32, (cl, 2 * cl), 1) % cl
    incl = (si >= ti) if reverse else (si <= ti)

    g1 = [_mm1(cat(kkg[p] * m1, rg[p] * m1), cat(bd[p], kd[p]), NT) for p in every]
    g2 = [_mm1(cat(kkg[p] * m2, rg[p] * m2), cat(kd[p], bd[p]), NT) for p in every]
    xs = [_mm1(cat(kkg[p], rg[p]), s0[p], NT) for p in every]
    gl = [jnp.where(strict, cat(g1[p][0:cl], g2[p][0:cl]), 0.0) for p in every]
    l_diag = [jnp.where(same, gl[p], 0.0) for p in every]
    l_anti = [jnp.where(same, 0.0, gl[p]) for p in every]
    lkv = [_mm1(l_anti[p], cat(vs[p] * m2, vs[p] * m1)) for p in every]

    inv = [eye - l_diag[p] for p in every]
    pwr = [_mm1(l_diag[p], l_diag[p]) for p in every]
    n_sq = int(math.log2(cl)) - 1
    for i in range(n_sq):
        inv = [inv[p] + _mm1(inv[p], pwr[p]) for p in every]
        if i + 1 < n_sq:
            pwr = [_mm1(pwr[p], pwr[p]) for p in every]

    rhs = [xs[p][0:cl] + lkv[p][0:cl] + lkv[p][cl:] for p in every]
    ust = [_mm1(inv[p], cat(rhs[p] * m1, rhs[p] * m2)) for p in every]
    u = [ust[p][0:cl] + ust[p][cl:] for p in every]
    o1 = [_mm1(jnp.where(incl, g1[p][cl:], 0.0), cat(-u[p] * m1, vs[p] * m1)) for p in every]
    o2 = [_mm1(jnp.where(incl, g2[p][cl:], 0.0), cat(vs[p] * m2, -u[p] * m2)) for p in every]
    upd = [_mm1(cat(vs[p], -u[p]), cat(kc[p], bc[p]), TN) for p in every]
    for p in every:
        o_ref[:, p * pw:(p + 1) * pw] = xs[p][cl:] + o1[p] + o2[p]
        st_ref[p] = s0[p] * decay[p] + jnp.where(same, upd[p], 0.0)


def _rwkv_scan(r, lw, k, v, kk, ab, lay, reverse):
    n, width = r.shape
    pairs = width // (2 * C_HEAD_DIM)
    steps = (lay.tc + lay.tl) // SCAN_CHUNK
    blk = pl.BlockSpec((SCAN_CHUNK, width), lambda b, s: (lay.scan_block(b, s, reverse), 0))
    return pl.pallas_call(
        functools.partial(_rwkv_kernel, reverse=reverse, pairs=pairs),
        grid=(lay.bsz, steps),
        in_specs=[blk] * 6,
        out_specs=blk,
        out_shape=jax.ShapeDtypeStruct((n, width), F32),
        scratch_shapes=[pltpu.VMEM((pairs, 2 * C_HEAD_DIM, 2 * C_HEAD_DIM), F32)],
        compiler_params=_params("parallel", "arbitrary"),
        name="rwkv_bwd" if reverse else "rwkv_fwd",
    )(r, lw, k, v, kk, ab)


def _outproj_kernel(x_ref, ya_ref, hf_ref, hb_ref, og_ref, hg_ref, rf_ref, rb_ref, bonus_ref, gate_ref,
                    gng_ref, gnb_ref, avg_ref, w_ref, g1_ref, n2_ref, sc2_ref, sh2_ref, rw_ref, rbias_ref,
                    xo_ref, h2_ref, lg_ref, *, a_width, b_width):
    o = hf_ref[...] + hb_ref[...]
    og = og_ref[...]
    yb = []
    for h in range(b_width // B_HEAD_DIM):
        lo, hi = h * B_HEAD_DIM, (h + 1) * B_HEAD_DIM
        yb.append(_rms(o[:, lo:hi]) * hg_ref[:, lo:hi] * _sigmoid(og[:, lo:hi]))
    yb = jnp.concatenate(yb, axis=1)

    rw = rf_ref[...] + rb_ref[...]
    avg = avg_ref[...]
    xc = rw - _mm_exact_rhs(rw, avg)
    var = _mm_exact_rhs(xc * xc, avg)
    yc = (xc * lax.rsqrt(var + RWKV_GN_EPS) * gng_ref[...] + gnb_ref[...] + bonus_ref[...]) * gate_ref[...]

    y = (_mm1(ya_ref[...], w_ref[0:a_width]) + _mm1(yb, w_ref[a_width:a_width + b_width])
         + _mm1(yc, w_ref[a_width + b_width:]))
    xn = x_ref[...] + g1_ref[...] * y
    xo_ref[...] = xn
    h2 = _rms(xn) * n2_ref[...] * (1.0 + sc2_ref[...]) + sh2_ref[...]
    h2_ref[...] = h2.astype(BF16)
    lg_ref[...] = _mm3(rw_ref[...], h2, NT) + rbias_ref[...]


def _outproj(x_all, ya, hf, hb, pb, hgrn_g, rf, rb, bonus, gate, gn_g, gn_b, w_out_bf16, mods, layer,
             norm2_g, router_wt, router_b_col, lay, tile_off, n_tiles):
    n, d = x_all.shape
    a_width = ya.shape[1]
    b_width = hf.shape[1]
    c_width = rf.shape[1]
    tm = TM_TOKEN
    hi = lax.broadcasted_iota(jnp.int32, (c_width, c_width), 0) // C_HEAD_DIM
    hj = lax.broadcasted_iota(jnp.int32, (c_width, c_width), 1) // C_HEAD_DIM
    avg = jnp.where(hi == hj, 1.0 / C_HEAD_DIM, 0.0).astype(BF16)
    tok = lambda w, comp=0, off=tile_off: pl.BlockSpec((tm, w), lambda i: (i + off, comp))
    full = lambda a: pl.BlockSpec(a.shape, lambda i: (0,) * a.ndim)
    row = lambda a: a.reshape(1, -1)
    mod = lambda part: _mod_spec(lay, tm, layer, part, d, tile_off, 0)
    hg, gg, gb, n2 = row(hgrn_g), row(gn_g), row(gn_b), row(norm2_g)
    rows = n_tiles * tm
    return pl.pallas_call(
        functools.partial(_outproj_kernel, a_width=a_width, b_width=b_width),
        grid=(n_tiles,),
        in_specs=[tok(d), tok(a_width, off=0), tok(b_width), tok(b_width), tok(b_width, 4), full(hg),
                  tok(c_width), tok(c_width), tok(c_width), tok(c_width), full(gg), full(gb), full(avg),
                  full(w_out_bf16), mod(2), full(n2), mod(4), mod(3), full(router_wt), full(router_b_col)],
        out_specs=[tok(d, off=0), tok(d, off=0), pl.BlockSpec((N_EXPERTS, tm), lambda i: (0, i))],
        out_shape=[jax.ShapeDtypeStruct((rows, d), F32), jax.ShapeDtypeStruct((rows, d), BF16),
                   jax.ShapeDtypeStruct((N_EXPERTS, rows), F32)],
        compiler_params=_params("parallel"),
        name="outproj",
    )(x_all, ya, hf, hb, pb, hg, rf, rb, bonus, gate, gg, gb, avg, w_out_bf16, mods, n2, mods, mods,
      router_wt, router_b_col)


def _route(logits):
    per_group = N_EXPERTS // N_EXPERT_GROUPS
    assert per_group == 4
    expert = lax.broadcasted_iota(jnp.int32, logits.shape, 0)
    ex = jnp.exp(logits - jnp.max(logits, axis=0, keepdims=True))
    p = ex / jnp.sum(ex, axis=0, keepdims=True)

    def top2_sum(a, b, c, d):
        h1, l1, h2, l2 = jnp.maximum(a, b), jnp.minimum(a, b), jnp.maximum(c, d), jnp.minimum(c, d)
        return jnp.maximum(h1, h2) + jnp.maximum(jnp.minimum(h1, h2), jnp.maximum(l1, l2))

    score = [top2_sum(*(p[g * per_group + i:g * per_group + i + 1] for i in range(per_group)))
             for g in range(N_EXPERT_GROUPS)]
    best, sel = score[0], jnp.zeros(score[0].shape, jnp.int32)
    for g in range(1, N_EXPERT_GROUPS):
        better = score[g] > best
        best = jnp.where(better, score[g], best)
        sel = jnp.where(better, g, sel)
    pm = jnp.where(expert // per_group == sel, p, -1.0)

    def first_max(vals):
        top = jnp.max(vals, axis=0, keepdims=True)
        return top, jnp.min(jnp.where(vals == top, expert, N_EXPERTS), axis=0, keepdims=True)

    p1, i1 = first_max(pm)
    p2, i2 = first_max(jnp.where(expert == i1, -2.0, pm))
    tot = p1 + p2
    gates = jnp.where(expert == i1, p1 / tot, 0.0) + jnp.where(expert == i2, p2 / tot, 0.0)
    return gates, (expert == i1) | (expert == i2)


def _moe_kernel(h_ref, lg_ref, wgu_ref, wd_ref, o_ref, gate_ref, rank_ref, *, d_expert, rows):
    e = pl.program_id(1)
    tm = h_ref.shape[0]

    @pl.when(e == 0)
    def _():
        o_ref[...] = jnp.zeros_like(o_ref)
        gates, chosen = _route(lg_ref[...])
        lane = lax.broadcasted_iota(jnp.int32, chosen.shape, 1)
        count = jnp.where(chosen, 1.0, 0.0)
        step = 1
        while step < tm:
            count = count + jnp.where(lane >= step, pltpu.roll(count, step, axis=1), 0.0)
            step *= 2
        gate_ref[...] = gates
        rank_ref[...] = jnp.where(chosen, count - 1.0, -1.0)

    g_row = gate_ref[pl.ds(e, 1), :]
    rank_row = rank_ref[pl.ds(e, 1), :]
    n_sel = jnp.max(rank_row, axis=1, keepdims=True)[0, 0].astype(jnp.int32) + 1

    def one_pass(j):
        slot = (lax.broadcasted_iota(jnp.int32, (rows, tm), 0) + j * rows).astype(F32)
        hit = rank_row == slot
        pick = jnp.where(hit, 1.0, 0.0).astype(BF16)
        hg = _dot(pick, h_ref[...]).astype(BF16)
        gu = _dot(hg, wgu_ref[...])
        gt, up = gu[:, :d_expert], gu[:, d_expert:]
        y = _mm1(gt * _sigmoid(gt) * up, wd_ref[...])
        gcol = jnp.sum(jnp.where(hit, g_row, 0.0), axis=1, keepdims=True)
        o_ref[...] += _dot(pick, (y * gcol).astype(BF16), TN)
        return j + 1

    lax.while_loop(lambda j: j * rows < n_sel, one_pass, jnp.int32(0))


def _moe(h2, logits_t, wgu_bf16, wd_bf16):
    n, d = h2.shape
    n_exp, _, two_de = wgu_bf16.shape
    tm = MOE_TM
    assert n % tm == 0
    once = pl.Buffered(1)
    return pl.pallas_call(
        functools.partial(_moe_kernel, d_expert=two_de // 2, rows=MOE_ROWS),
        grid=(n // tm, n_exp),
        in_specs=[pl.BlockSpec((tm, d), lambda i, e: (i, 0), pipeline_mode=once),
                  pl.BlockSpec((n_exp, tm), lambda i, e: (0, i)),
                  pl.BlockSpec((None, d, two_de), lambda i, e: (e, 0, 0)),
                  pl.BlockSpec((None, two_de // 2, d), lambda i, e: (e, 0, 0))],
        out_specs=pl.BlockSpec((tm, d), lambda i, e: (i, 0), pipeline_mode=once),
        out_shape=jax.ShapeDtypeStruct((n, d), F32),
        scratch_shapes=[pltpu.VMEM((n_exp, tm), F32), pltpu.VMEM((n_exp, tm), F32)],
        compiler_params=_params("parallel", "arbitrary"),
        name="moe",
    )(h2, logits_t, wgu_bf16, wd_bf16)


def _residual_kernel(x_ref, f_ref, g2_ref, fg_ref, o_ref, *, final_norm):
    xn = x_ref[...] + g2_ref[...] * f_ref[...]
    o_ref[...] = _rms(xn) * fg_ref[...] if final_norm else xn


def _residual(x_mid, f, mods, layer, final_g, lay, tile_off, final_norm):
    n, d = x_mid.shape
    tm = TM_TOKEN
    tok = pl.BlockSpec((tm, d), lambda i: (i, 0))
    return pl.pallas_call(
        functools.partial(_residual_kernel, final_norm=final_norm),
        grid=(n // tm,),
        in_specs=[tok, tok, _mod_spec(lay, tm, layer, 5, d, tile_off, 0), pl.BlockSpec((1, d), lambda i: (0, 0))],
        out_specs=tok,
        out_shape=jax.ShapeDtypeStruct((n, d), F32),
        compiler_params=_params("parallel"),
        name="residual",
    )(x_mid, f, mods, final_g.reshape(1, d))


def kernel(x, c, ctx, c_ctx, ada_w, ada_b, norm1_g, norm2_g, w_in, w_out, gmlp_norm_g, gmlp_ws, gmlp_b,
           hgrn_lb_logits, hgrn_norm_g, rwkv_mu, rwkv_w0, rwkv_w_up, rwkv_a0, rwkv_a_up, rwkv_g_up,
           rwkv_k_k, rwkv_k_a, rwkv_r_k, rwkv_gn_g, rwkv_gn_b, router_w, router_b, moe_w_gate_up,
           moe_w_down, final_norm_g):
    bsz, tl, d = x.shape
    tc = ctx.shape[1]
    depth = ada_w.shape[0]
    lay = _Layout(bsz, tc, tl)
    a_cols = 2 * gmlp_norm_g.shape[1]
    b_cols = 5 * hgrn_norm_g.shape[1]
    c_width = rwkv_a0.shape[1]
    assert lay.n_ctx % TM_MATMUL == 0 and tl % TM_MATMUL == 0 and tc % SCAN_CHUNK == 0

    mod_rows = -(-(bsz + 1) // 8) * 8
    c_rows = jnp.zeros((mod_rows, d), F32).at[:bsz].set(c).at[bsz].set(c_ctx)
    mods = _adaln(c_rows, ada_w, ada_b).reshape(depth, mod_rows, 6, 1, d)

    lbs = jnp.cumsum(jax.nn.softmax(hgrn_lb_logits.astype(F32), axis=1), axis=1)
    lbs = lbs - lbs[:, :1]
    w_in_bf16 = w_in.astype(BF16)
    w_out_bf16 = w_out.astype(BF16)
    wgu_bf16 = moe_w_gate_up.astype(BF16)
    wd_bf16 = moe_w_down.astype(BF16)
    router_wt = router_w.T
    router_b_col = router_b.reshape(N_EXPERTS, 1)

    x_all = jnp.concatenate([ctx.reshape(lay.n_ctx, d), x.reshape(lay.n_lat, d)], axis=0)
    ctx_tiles = lay.n_ctx // TM_TOKEN
    all_tiles = lay.n // TM_TOKEN

    for l in range(depth):
        last = l == depth - 1
        tile_off = ctx_tiles if last else 0
        n_tiles = all_tiles - tile_off
        w_l = w_in_bf16[l]
        pa = _inproj(x_all, norm1_g[l], mods, l, w_l[:, :a_cols], a_cols, lay)
        pb = _inproj(x_all, norm1_g[l], mods, l, w_l[:, a_cols:a_cols + b_cols], b_cols // 3, lay)
        pc = _inproj(x_all, norm1_g[l], mods, l, w_l[:, a_cols + b_cols:], (w_l.shape[1] - a_cols - b_cols) // 2, lay)

        ya = _gmlp(pa, gmlp_norm_g[l], gmlp_ws[l], gmlp_b[l], tile_off, n_tiles)
        hf = _hgrn(pb, lbs[0, l], lay, reverse=False)
        hb = _hgrn(pb, lbs[1, l], lay, reverse=True)

        w_lora = jnp.zeros((DECAY_LORA + AAA_LORA, 3 * c_width), F32)
        w_lora = w_lora.at[:DECAY_LORA, :c_width].set(rwkv_w_up[l, 0])
        w_lora = w_lora.at[:DECAY_LORA, c_width:2 * c_width].set(rwkv_w_up[l, 1])
        w_lora = w_lora.at[DECAY_LORA:, 2 * c_width:].set(rwkv_a_up[l])
        r, lwf, lwb, k2, v, kk, ab, gate, bonus = _rwkv_prep(
            pc, lay, rwkv_mu[l], w_lora, rwkv_g_up[l], rwkv_w0[l], rwkv_a0[l], rwkv_k_k[l], rwkv_k_a[l],
            rwkv_r_k[l])
        rf = _rwkv_scan(r, lwf, k2, v, kk, ab, lay, reverse=False)
        rb = _rwkv_scan(r, lwb, k2, v, kk, ab, lay, reverse=True)

        x_mid, h2, logits_t = _outproj(
            x_all, ya, hf, hb, pb, hgrn_norm_g[l], rf, rb, bonus, gate, rwkv_gn_g[l], rwkv_gn_b[l],
            w_out_bf16[l], mods, l, norm2_g[l], router_wt, router_b_col, lay, tile_off, n_tiles)
        f = _moe(h2, logits_t, wgu_bf16[l], wd_bf16[l])
        x_all = _residual(x_mid, f, mods, l, final_norm_g, lay, tile_off, final_norm=last)
    return x_all.reshape(bsz, tl, d)
```

```python
import functools
import math

import jax
import jax.numpy as jnp
from jax import lax
from jax.experimental import pallas as pl
from jax.experimental.pallas import tpu as pltpu

F32 = jnp.float32
BF16 = jnp.bfloat16

GRID_W = 64
A_HEAD_DIM = 128
B_HEAD_DIM = 128
C_HEAD_DIM = 64
GMLP_CHUNK = 128
DECAY_LORA = 64
AAA_LORA = 64
GATE_LORA = 128
N_EXPERTS = 16
N_EXPERT_GROUPS = 4
NORM_EPS = 1e-6
RWKV_GN_EPS = 64e-5
DECAY_SCALE = math.exp(-0.5)

LANES = 128
MXU_TILE = 256
SCAN_CHUNK = 64
HGRN_SUB = 8
TM_MATMUL = 512
TM_TOKEN = 256
MOE_TM = 1024
MOE_ROWS = 160
VMEM_LIMIT = 56 * 1024 * 1024

NN = (((1,), (0,)), ((), ()))
NT = (((1,), (1,)), ((), ()))
TN = (((0,), (0,)), ((), ()))


def _dot(a, b, dims=NN):
    return lax.dot_general(a, b, dims, preferred_element_type=F32)


def _mm1(a, b, dims=NN):
    return _dot(a.astype(BF16), b.astype(BF16), dims)


def _split2(x):
    hi = x.astype(BF16)
    lo = (x - hi.astype(F32)).astype(BF16)
    return hi, lo


def _split3(x):
    hi = x.astype(BF16)
    r1 = x - hi.astype(F32)
    mid = r1.astype(BF16)
    lo = (r1 - mid.astype(F32)).astype(BF16)
    return hi, mid, lo


def _mm3(a, b, dims=NN):
    ah, al = _split2(a)
    bh, bl = _split2(b)
    return _dot(ah, bh, dims) + (_dot(al, bh, dims) + _dot(ah, bl, dims))


def _mm3_split_rhs(a, bh, bl):
    ah, al = _split2(a)
    return _dot(ah, bh) + (_dot(al, bh) + _dot(ah, bl))


def _mm_exact_lhs(a_bf16, b, dims=NN):
    h, m, l = _split3(b)
    return _dot(a_bf16, h, dims) + (_dot(a_bf16, m, dims) + _dot(a_bf16, l, dims))


def _head_sums(x, seg):
    hi, lo = _split2(x)
    w = seg.shape[0]
    return jnp.concatenate(
        [_dot(hi[:, g:g + w], seg) + _dot(lo[:, g:g + w], seg) for g in range(0, x.shape[1], w)], axis=1)


def _head_block_diag(head_dim, value):
    hi = lax.broadcasted_iota(jnp.int32, (MXU_TILE, MXU_TILE), 0) // head_dim
    hj = lax.broadcasted_iota(jnp.int32, (MXU_TILE, MXU_TILE), 1) // head_dim
    return jnp.where(hi == hj, value, 0.0).astype(BF16)


def _sigmoid(x):
    return jax.nn.sigmoid(x)


def _rms(x):
    return x * lax.rsqrt(jnp.mean(x * x, axis=-1, keepdims=True) + NORM_EPS)


def _params(*sem):
    return pltpu.CompilerParams(dimension_semantics=sem, vmem_limit_bytes=VMEM_LIMIT)


def _adaln_kernel(c_ref, w_ref, b_ref, o_ref):
    c = c_ref[...]
    o_ref[...] = _mm3(c * _sigmoid(c), w_ref[...]) + b_ref[...]


def _adaln(c_rows, ada_w, ada_b):
    depth, d, n = ada_w.shape
    rows = c_rows.shape[0]
    tn = 1024
    return pl.pallas_call(
        _adaln_kernel,
        grid=(depth, n // tn),
        in_specs=[
            pl.BlockSpec((rows, d), lambda l, j: (0, 0)),
            pl.BlockSpec((None, d, tn), lambda l, j: (l, 0, j)),
            pl.BlockSpec((None, 1, tn), lambda l, j: (l, 0, j)),
        ],
        out_specs=pl.BlockSpec((None, rows, tn), lambda l, j: (l, 0, j)),
        out_shape=jax.ShapeDtypeStruct((depth, rows, n), F32),
        compiler_params=_params("parallel", "parallel"),
        name="adaln",
    )(c_rows, ada_w, ada_b.reshape(depth, 1, n))


class _Layout:
    def __init__(self, bsz, tc, tl):
        self.bsz, self.tc, self.tl = bsz, tc, tl
        self.n_ctx = bsz * tc
        self.n_lat = bsz * tl
        self.n = self.n_ctx + self.n_lat

    def mod_row(self, tile, tm):
        ctx_tiles = self.n_ctx // tm
        return jnp.where(tile < ctx_tiles, self.bsz, (tile - ctx_tiles) // (self.tl // tm))

    def scan_block(self, b, step, reverse):
        ncc = self.tc // SCAN_CHUNK
        ncl = self.tl // SCAN_CHUNK
        if reverse:
            mc = ncc - 1 - step
            ml = ncl - 1 - (step - ncc)
        else:
            mc = step
            ml = step - ncc
        return jnp.where(step < ncc, b * ncc + mc, self.bsz * ncc + b * ncl + ml)


def _mod_spec(lay, tm, layer, part, d, tile_off, grid_pos):
    def index(*ids):
        return (layer, lay.mod_row(ids[grid_pos] + tile_off, tm), part, 0, 0)

    return pl.BlockSpec((None, None, None, 1, d), index)


def _inproj_kernel(x_ref, g_ref, sc_ref, sh_ref, w_ref, o_ref):
    h = _rms(x_ref[...]) * g_ref[...] * (1.0 + sc_ref[...]) + sh_ref[...]
    o_ref[...] = _mm1(h, w_ref[...])


def _inproj(x_all, norm_g, mods, layer, w_bf16, tn, lay):
    n, d = x_all.shape
    cols = w_bf16.shape[1]
    tm = TM_MATMUL
    return pl.pallas_call(
        _inproj_kernel,
        grid=(cols // tn, n // tm),
        in_specs=[
            pl.BlockSpec((tm, d), lambda j, i: (i, 0)),
            pl.BlockSpec((1, d), lambda j, i: (0, 0)),
            _mod_spec(lay, tm, layer, 1, d, 0, 1),
            _mod_spec(lay, tm, layer, 0, d, 0, 1),
            pl.BlockSpec((d, tn), lambda j, i: (0, j)),
        ],
        out_specs=pl.BlockSpec((tm, tn), lambda j, i: (i, j)),
        out_shape=jax.ShapeDtypeStruct((n, cols), F32),
        compiler_params=_params("parallel", "parallel"),
        name="inproj",
    )(x_all, norm_g.reshape(1, d), mods, mods, w_bf16)


def _gmlp_kernel(p_ref, g_ref, ws_ref, wbt_ref, o_ref, *, heads):
    x = p_ref[...]
    ge = 0.5 * x * (1.0 + lax.erf(x * (1.0 / math.sqrt(2.0))))
    width = heads * A_HEAD_DIM
    tm = x.shape[0]
    for h in range(heads):
        lo, hi = h * A_HEAD_DIM, (h + 1) * A_HEAD_DIM
        vn = _rms(ge[:, width + lo:width + hi]) * g_ref[:, lo:hi]
        for c in range(tm // GMLP_CHUNK):
            r0, r1 = c * GMLP_CHUNK, (c + 1) * GMLP_CHUNK
            s = _mm1(ws_ref[h], vn[r0:r1]) + wbt_ref[:, h:h + 1]
            o_ref[r0:r1, lo:hi] = ge[r0:r1, lo:hi] * s


def _gmlp(pa, norm_g, ws, wb, tile_off, n_tiles):
    n, cols = pa.shape
    heads = ws.shape[0]
    width = cols // 2
    tm = TM_TOKEN
    return pl.pallas_call(
        functools.partial(_gmlp_kernel, heads=heads),
        grid=(n_tiles,),
        in_specs=[
            pl.BlockSpec((tm, cols), lambda i: (i + tile_off, 0)),
            pl.BlockSpec((1, width), lambda i: (0, 0)),
            pl.BlockSpec((heads, GMLP_CHUNK, GMLP_CHUNK), lambda i: (0, 0, 0)),
            pl.BlockSpec((GMLP_CHUNK, heads), lambda i: (0, 0)),
        ],
        out_specs=pl.BlockSpec((tm, width), lambda i: (i, 0)),
        out_shape=jax.ShapeDtypeStruct((n_tiles * tm, width), F32),
        compiler_params=_params("parallel"),
        name="gmlp",
    )(pa, norm_g.reshape(1, width), ws.astype(BF16), wb.T)


def _hgrn_kernel(qf_ref, if_ref, zf_ref, qb_ref, ib_ref, zb_ref, lb_ref, yf_ref, yb_ref, st_ref, *, heads):
    @pl.when(pl.program_id(1) == 0)
    def _():
        st_ref[...] = jnp.zeros_like(st_ref)

    cl, sub, hd = SCAN_CHUNK, HGRN_SUB, B_HEAD_DIM
    row = lax.broadcasted_iota(jnp.int32, (cl, cl), 0)
    col = lax.broadcasted_iota(jnp.int32, (cl, cl), 1)
    dirs = []
    for z, (q_ref, i_ref, z_ref) in enumerate(((qf_ref, if_ref, zf_ref), (qb_ref, ib_ref, zb_ref))):
        reverse = z == 1
        q = q_ref[...]
        q = q * _sigmoid(q)
        v = i_ref[...]
        zz = z_ref[...]
        lb = lb_ref[z:z + 1]
        pos = zz > 0.0
        t = jnp.exp(-jnp.abs(zz))
        den = 1.0 + t
        k = (1.0 - lb) * jnp.where(pos, t, 1.0) / den
        num = jnp.where(pos, 1.0 + lb * t, lb + t)
        ok = num > 0.0
        lf = jnp.where(ok, jnp.log(jnp.where(ok, num, 1.0)), zz) - jnp.log(den)

        ut = (cl - 1 - row) if reverse else row
        us = (cl - 1 - col) if reverse else col
        sel = [us <= ut]
        keeps = []
        half = cl // 2
        while half >= sub:
            mid = (ut // (2 * half)) * (2 * half) + half
            sel.append(us < mid)
            keeps.append((ut >= mid) & (us < mid) & (us >= mid - half))
            half //= 2
        start = (ut // sub) * sub
        sel.append(us < start)
        keeps.append((us >= start) & (us <= ut))
        cums = _mm_exact_lhs(jnp.concatenate(sel, axis=0).astype(BF16), lf)
        c = cums[0:cl]
        total = c[0:1] if reverse else c[cl - 1:cl]
        dirs.append(dict(q=q, k=k, v=v, c=c, cums=cums, total=total, keeps=keeps,
                         qe=q * jnp.exp(c), ke=k * jnp.exp(total - c), scores=[None] * heads))

    for lvl in range(len(dirs[0]["keeps"])):
        for d in dirs:
            ref = d["cums"][(lvl + 1) * cl:(lvl + 2) * cl]
            dq, dk = d["c"] - ref, ref - d["c"]
            if lvl + 1 < len(d["keeps"]):
                dq, dk = jnp.minimum(dq, 0.0), jnp.minimum(dk, 0.0)
            qs = d["q"] * jnp.exp(dq)
            ks = d["k"] * jnp.exp(dk)
            for h in range(heads):
                lo, hi = h * hd, (h + 1) * hd
                s = jnp.where(d["keeps"][lvl], _mm1(qs[:, lo:hi], ks[:, lo:hi], NT), 0.0)
                d["scores"][h] = s if d["scores"][h] is None else d["scores"][h] + s

    for z, (d, y_ref) in enumerate(zip(dirs, (yf_ref, yb_ref))):
        for h in range(heads):
            lo, hi = h * hd, (h + 1) * hd
            st = st_ref[z, h]
            y_ref[:, lo:hi] = _mm1(d["qe"][:, lo:hi], st, NT) + _mm1(d["scores"][h], d["v"][:, lo:hi])
            st_ref[z, h] = (st * jnp.exp(d["total"][:, lo:hi])
                            + _mm1(d["v"][:, lo:hi], d["ke"][:, lo:hi], TN))


def _hgrn(pb, lb_dirs, lay):
    n = pb.shape[0]
    width = pb.shape[1] // 5
    heads = width // B_HEAD_DIM
    steps = (lay.tc + lay.tl) // SCAN_CHUNK
    blk = lambda comp, reverse: pl.BlockSpec(
        (SCAN_CHUNK, width), lambda b, s: (lay.scan_block(b, s, reverse), comp))
    out = jax.ShapeDtypeStruct((n, width), F32)
    return pl.pallas_call(
        functools.partial(_hgrn_kernel, heads=heads),
        grid=(lay.bsz, steps),
        in_specs=[blk(0, False), blk(1, False), blk(2, False), blk(0, True), blk(1, True), blk(3, True),
                  pl.BlockSpec((2, width), lambda b, s: (0, 0))],
        out_specs=[blk(0, False), blk(0, True)],
        out_shape=[out, out],
        scratch_shapes=[pltpu.VMEM((2, heads, B_HEAD_DIM, B_HEAD_DIM), F32)],
        compiler_params=_params("parallel", "arbitrary"),
        name="hgrn",
    )(pb, pb, pb, pb, pb, pb, lb_dirs)


def _rwkv_prep_kernel(cur_ref, prev_ref, next_ref, mu_ref, wlh_ref, wll_ref, guph_ref, gupl_ref, w0f_ref, w0b_ref, a0_ref,
                      kk_ref, ka_ref, rk_ref, seg_ref,
                      r_out, lwf_out, lwb_out, k_out, v_out, kk_out, ab_out, g_out, bonus_out,
                      *, ctx_tiles, rows_per_batch, width):
    tile = pl.program_id(0)
    z = cur_ref[...]
    tm, cols = z.shape
    halo = prev_ref.shape[0]
    ext = jnp.concatenate([prev_ref[...], z, next_ref[...]], axis=0)
    left = ext[halo - 1:halo - 1 + tm]
    right = ext[halo + 1:halo + 1 + tm]
    up = ext[0:tm]
    down = ext[2 * halo:2 * halo + tm]

    is_ctx = tile < ctx_tiles
    t = lax.broadcasted_iota(jnp.int32, (tm, 1), 0)
    rowb = t + ((tile - ctx_tiles) * tm) % rows_per_batch
    line = jnp.where(is_ctx, tm, GRID_W)
    pos = t & (line - 1)
    left = jnp.where(pos != 0, left, 0.0)
    right = jnp.where(pos != line - 1, right, 0.0)
    up = jnp.where(rowb >= GRID_W, up, 0.0)
    down = jnp.where(rowb < rows_per_batch - GRID_W, down, 0.0)
    slot = lax.broadcasted_iota(jnp.int32, (1, cols), 1) % 4
    slot = jnp.where(is_ctx, slot % 2, slot)
    shifted = jnp.where(slot == 0, left, jnp.where(slot == 1, right, jnp.where(slot == 2, up, down)))
    xx = z + mu_ref[...] * (shifted - z)

    r = xx[:, 0:width]
    k = xx[:, width:2 * width]
    v = xx[:, 2 * width:3 * width]
    lora = xx[:, 3 * width:3 * width + DECAY_LORA + AAA_LORA]
    lane = lax.broadcasted_iota(jnp.int32, (1, DECAY_LORA + AAA_LORA), 1)
    lora = jnp.where(lane < DECAY_LORA, jnp.tanh(lora), lora)
    up3 = _mm3_split_rhs(lora, wlh_ref[...], wll_ref[...])
    gd = xx[:, 3 * width + DECAY_LORA + AAA_LORA:]
    g_out[...] = _mm3_split_rhs(_sigmoid(gd), guph_ref[...], gupl_ref[...])
    lwf_out[...] = -DECAY_SCALE * _sigmoid(w0f_ref[...] + up3[:, 0:width])
    lwb_out[...] = -DECAY_SCALE * _sigmoid(w0b_ref[...] + up3[:, width:2 * width])
    a = _sigmoid(a0_ref[...] + up3[:, 2 * width:3 * width])

    seg = seg_ref[...]
    kk = k * kk_ref[...]
    kk = kk / jnp.maximum(jnp.sqrt(_head_sums(kk * kk, seg)), 1e-12)
    k2 = k * (1.0 + (a - 1.0) * ka_ref[...])
    r_out[...] = r
    k_out[...] = k2
    v_out[...] = v
    kk_out[...] = kk
    ab_out[...] = kk * a
    bonus_out[...] = _head_sums(r * k2 * rk_ref[...], seg) * v


def _rwkv_prep(pc, lay, mu, w_lora, g_up, w0, a0, k_k, k_a, r_k):
    n, cols = pc.shape
    width = a0.shape[-1]
    tm = TM_TOKEN
    assert lay.tc == tm and lay.tl % tm == 0 and tm % GRID_W == 0
    halo = GRID_W
    hb = tm // halo
    last = n // halo - 1
    row = lambda a: a.reshape(1, -1)
    seg = _head_block_diag(C_HEAD_DIM, 1.0)
    assert width % MXU_TILE == 0
    full = lambda a: pl.BlockSpec(a.shape, lambda i: (0,) * a.ndim)
    consts = [row(mu), *_split2(w_lora), *_split2(g_up), row(w0[0]), row(w0[1]), row(a0), row(k_k), row(k_a),
              row(r_k), seg]
    out_spec = pl.BlockSpec((tm, width), lambda i: (i, 0))
    return pl.pallas_call(
        functools.partial(_rwkv_prep_kernel, ctx_tiles=lay.n_ctx // tm, rows_per_batch=lay.tl, width=width),
        grid=(n // tm,),
        in_specs=[
            pl.BlockSpec((tm, cols), lambda i: (i, 0)),
            pl.BlockSpec((halo, cols), lambda i: (jnp.maximum(i * hb - 1, 0), 0)),
            pl.BlockSpec((halo, cols), lambda i: (jnp.minimum((i + 1) * hb, last), 0)),
        ] + [full(a) for a in consts],
        out_specs=[out_spec] * 9,
        out_shape=[jax.ShapeDtypeStruct((n, width), F32)] * 9,
        compiler_params=_params("parallel"),
        name="rwkv_prep",
    )(pc, pc, pc, *consts)


def _rwkv_kernel(rf_ref, kf_ref, vf_ref, kkf_ref, abf_ref, lwf_ref, rb_ref, kb_ref, vb_ref, kkb_ref, abb_ref,
                 lwb_ref, of_ref, ob_ref, st_ref, *, pairs):
    @pl.when(pl.program_id(1) == 0)
    def _():
        st_ref[...] = jnp.zeros_like(st_ref)

    cl, hd, pw = SCAN_CHUNK, C_HEAD_DIM, 2 * C_HEAD_DIM
    cat = lambda a, b: jnp.concatenate([a, b], axis=0)
    row = lax.broadcasted_iota(jnp.int32, (cl, cl), 0)
    col = lax.broadcasted_iota(jnp.int32, (cl, cl), 1)
    lane = lax.broadcasted_iota(jnp.int32, (cl, pw), 1)
    m1 = (lane < hd).astype(F32)
    m2 = 1.0 - m1
    r2 = lax.broadcasted_iota(jnp.int32, (2 * cl, 2 * cl), 0)
    c2 = lax.broadcasted_iota(jnp.int32, (2 * cl, 2 * cl), 1)
    tt, ss = r2 % cl, c2 % cl
    same = (r2 // cl) == (c2 // cl)
    eye = jnp.where(r2 == c2, 1.0, 0.0)
    ti = lax.broadcasted_iota(jnp.int32, (cl, 2 * cl), 0)
    si = lax.broadcasted_iota(jnp.int32, (cl, 2 * cl), 1) % cl

    kkg, rg, kd, bd, kc, bc, vs, decay, s0, strict, incl = ([] for _ in range(11))
    for z, refs in enumerate(((rf_ref, kf_ref, vf_ref, kkf_ref, abf_ref, lwf_ref),
                              (rb_ref, kb_ref, vb_ref, kkb_ref, abb_ref, lwb_ref))):
        reverse = z == 1
        cut = lambda x: [x[:, p * pw:(p + 1) * pw] for p in range(pairs)]
        r, k, v, kk, ab, lw = (ref[...] for ref in refs)
        tri = (col >= row) if reverse else (col <= row)
        c = _mm_exact_lhs(tri.astype(BF16), lw)
        total = c[0:1] if reverse else c[cl - 1:cl]
        e_out = jnp.exp(-c)
        e_last = jnp.exp(total - c)
        kkg += cut(kk * jnp.exp(c - lw))
        rg += cut(r * jnp.exp(c))
        kd += cut(k * e_out)
        bd += cut(ab * e_out)
        kc += cut(k * e_last)
        bc += cut(ab * e_last)
        vs += cut(v)
        decay += cut(jnp.exp(total))
        s0 += [st_ref[z, p] for p in range(pairs)]
        strict += [(ss > tt) if reverse else (ss < tt)] * pairs
        incl += [(si >= ti) if reverse else (si <= ti)] * pairs
    every = range(2 * pairs)

    g1 = [_mm1(cat(kkg[p] * m1, rg[p] * m1), cat(bd[p], kd[p]), NT) for p in every]
    g2 = [_mm1(cat(kkg[p] * m2, rg[p] * m2), cat(kd[p], bd[p]), NT) for p in every]
    xs = [_mm1(cat(kkg[p], rg[p]), s0[p], NT) for p in every]
    gl = [jnp.where(strict[p], cat(g1[p][0:cl], g2[p][0:cl]), 0.0) for p in every]
    l_diag = [jnp.where(same, gl[p], 0.0) for p in every]
    l_anti = [jnp.where(same, 0.0, gl[p]) for p in every]
    lkv = [_mm1(l_anti[p], cat(vs[p] * m2, vs[p] * m1)) for p in every]

    inv = [eye - l_diag[p] for p in every]
    pwr = [_mm1(l_diag[p], l_diag[p]) for p in every]
    n_sq = int(math.log2(cl)) - 1
    for i in range(n_sq):
        inv = [inv[p] + _mm1(inv[p], pwr[p]) for p in every]
        if i + 1 < n_sq:
            pwr = [_mm1(pwr[p], pwr[p]) for p in every]

    rhs = [xs[p][0:cl] + lkv[p][0:cl] + lkv[p][cl:] for p in every]
    ust = [_mm1(inv[p], cat(rhs[p] * m1, rhs[p] * m2)) for p in every]
    u = [ust[p][0:cl] + ust[p][cl:] for p in every]
    o1 = [_mm1(jnp.where(incl[p], g1[p][cl:], 0.0), cat(-u[p] * m1, vs[p] * m1)) for p in every]
    o2 = [_mm1(jnp.where(incl[p], g2[p][cl:], 0.0), cat(vs[p] * m2, -u[p] * m2)) for p in every]
    upd = [_mm1(cat(vs[p], -u[p]), cat(kc[p], bc[p]), TN) for p in every]
    for p in every:
        z, q = divmod(p, pairs)
        o_ref = ob_ref if z else of_ref
        o_ref[:, q * pw:(q + 1) * pw] = xs[p][cl:] + o1[p] + o2[p]
        st_ref[z, q] = s0[p] * decay[p] + jnp.where(same, upd[p], 0.0)


def _rwkv_scan(r, lwf, lwb, k, v, kk, ab, lay):
    n, width = r.shape
    pairs = width // (2 * C_HEAD_DIM)
    steps = (lay.tc + lay.tl) // SCAN_CHUNK
    blk = lambda reverse: pl.BlockSpec(
        (SCAN_CHUNK, width), lambda b, s: (lay.scan_block(b, s, reverse), 0))
    out = jax.ShapeDtypeStruct((n, width), F32)
    return pl.pallas_call(
        functools.partial(_rwkv_kernel, pairs=pairs),
        grid=(lay.bsz, steps),
        in_specs=[blk(False)] * 6 + [blk(True)] * 6,
        out_specs=[blk(False), blk(True)],
        out_shape=[out, out],
        scratch_shapes=[pltpu.VMEM((2, pairs, 2 * C_HEAD_DIM, 2 * C_HEAD_DIM), F32)],
        compiler_params=_params("parallel", "arbitrary"),
        name="rwkv",
    )(r, k, v, kk, ab, lwf, r, k, v, kk, ab, lwb)


def _outproj_kernel(x_ref, ya_ref, hf_ref, hb_ref, og_ref, hg_ref, rf_ref, rb_ref, bonus_ref, gate_ref,
                    gng_ref, gnb_ref, avg_ref, w_ref, g1_ref, n2_ref, sc2_ref, sh2_ref, rw_ref, rbias_ref,
                    xo_ref, h2_ref, lg_ref, *, a_width, b_width):
    o = hf_ref[...] + hb_ref[...]
    og = og_ref[...]
    yb = []
    for h in range(b_width // B_HEAD_DIM):
        lo, hi = h * B_HEAD_DIM, (h + 1) * B_HEAD_DIM
        yb.append(_rms(o[:, lo:hi]) * hg_ref[:, lo:hi] * _sigmoid(og[:, lo:hi]))
    yb = jnp.concatenate(yb, axis=1)

    rw = rf_ref[...] + rb_ref[...]
    avg = avg_ref[...]
    xc = rw - _head_sums(rw, avg)
    var = _head_sums(xc * xc, avg)
    yc = (xc * lax.rsqrt(var + RWKV_GN_EPS) * gng_ref[...] + gnb_ref[...] + bonus_ref[...]) * gate_ref[...]

    y = (_mm1(ya_ref[...], w_ref[0:a_width]) + _mm1(yb, w_ref[a_width:a_width + b_width])
         + _mm1(yc, w_ref[a_width + b_width:]))
    xn = x_ref[...] + g1_ref[...] * y
    xo_ref[...] = xn
    h2 = _rms(xn) * n2_ref[...] * (1.0 + sc2_ref[...]) + sh2_ref[...]
    h2_ref[...] = h2.astype(BF16)
    lg_ref[...] = _mm3(rw_ref[...], h2, NT) + rbias_ref[...]


def _outproj(x_all, ya, hf, hb, pb, hgrn_g, rf, rb, bonus, gate, gn_g, gn_b, w_out_bf16, mods, layer,
             norm2_g, router_wt, router_b_col, lay, tile_off, n_tiles):
    n, d = x_all.shape
    a_width = ya.shape[1]
    b_width = hf.shape[1]
    c_width = rf.shape[1]
    tm = TM_TOKEN
    avg = _head_block_diag(C_HEAD_DIM, 1.0 / C_HEAD_DIM)
    assert c_width % MXU_TILE == 0
    tok = lambda w, comp=0, off=tile_off: pl.BlockSpec((tm, w), lambda i: (i + off, comp))
    full = lambda a: pl.BlockSpec(a.shape, lambda i: (0,) * a.ndim)
    row = lambda a: a.reshape(1, -1)
    mod = lambda part: _mod_spec(lay, tm, layer, part, d, tile_off, 0)
    hg, gg, gb, n2 = row(hgrn_g), row(gn_g), row(gn_b), row(norm2_g)
    rows = n_tiles * tm
    return pl.pallas_call(
        functools.partial(_outproj_kernel, a_width=a_width, b_width=b_width),
        grid=(n_tiles,),
        in_specs=[tok(d), tok(a_width, off=0), tok(b_width), tok(b_width), tok(b_width, 4), full(hg),
                  tok(c_width), tok(c_width), tok(c_width), tok(c_width), full(gg), full(gb), full(avg),
                  full(w_out_bf16), mod(2), full(n2), mod(4), mod(3), full(router_wt), full(router_b_col)],
        out_specs=[tok(d, off=0), tok(d, off=0), pl.BlockSpec((N_EXPERTS, tm), lambda i: (0, i))],
        out_shape=[jax.ShapeDtypeStruct((rows, d), F32), jax.ShapeDtypeStruct((rows, d), BF16),
                   jax.ShapeDtypeStruct((N_EXPERTS, rows), F32)],
        compiler_params=_params("parallel"),
        name="outproj",
    )(x_all, ya, hf, hb, pb, hg, rf, rb, bonus, gate, gg, gb, avg, w_out_bf16, mods, n2, mods, mods,
      router_wt, router_b_col)


def _route(logits):
    per_group = N_EXPERTS // N_EXPERT_GROUPS
    assert per_group == 4
    expert = lax.broadcasted_iota(jnp.int32, logits.shape, 0)
    ex = jnp.exp(logits - jnp.max(logits, axis=0, keepdims=True))
    p = ex / jnp.sum(ex, axis=0, keepdims=True)

    def top2_sum(a, b, c, d):
        h1, l1, h2, l2 = jnp.maximum(a, b), jnp.minimum(a, b), jnp.maximum(c, d), jnp.minimum(c, d)
        return jnp.maximum(h1, h2) + jnp.maximum(jnp.minimum(h1, h2), jnp.maximum(l1, l2))

    score = [top2_sum(*(p[g * per_group + i:g * per_group + i + 1] for i in range(per_group)))
             for g in range(N_EXPERT_GROUPS)]
    best, sel = score[0], jnp.zeros(score[0].shape, jnp.int32)
    for g in range(1, N_EXPERT_GROUPS):
        better = score[g] > best
        best = jnp.where(better, score[g], best)
        sel = jnp.where(better, g, sel)
    pm = jnp.where(expert // per_group == sel, p, -1.0)

    def first_max(vals):
        top = jnp.max(vals, axis=0, keepdims=True)
        return top, jnp.min(jnp.where(vals == top, expert, N_EXPERTS), axis=0, keepdims=True)

    p1, i1 = first_max(pm)
    p2, i2 = first_max(jnp.where(expert == i1, -2.0, pm))
    tot = p1 + p2
    gates = jnp.where(expert == i1, p1 / tot, 0.0) + jnp.where(expert == i2, p2 / tot, 0.0)
    return gates, (expert == i1) | (expert == i2)


def _moe_kernel(h_ref, lg_ref, wgu_ref, wd_ref, o_ref, gate_ref, rank_ref, *, d_expert, rows):
    e = pl.program_id(1)
    tm = h_ref.shape[0]

    @pl.when(e == 0)
    def _():
        o_ref[...] = jnp.zeros_like(o_ref)
        gates, chosen = _route(lg_ref[...])
        lane = lax.broadcasted_iota(jnp.int32, chosen.shape, 1)
        count = jnp.where(chosen, 1.0, 0.0)
        step = 1
        while step < tm:
            count = count + jnp.where(lane >= step, pltpu.roll(count, step, axis=1), 0.0)
            step *= 2
        gate_ref[...] = gates
        rank_ref[...] = jnp.where(chosen, count - 1.0, -1.0)

    g_row = gate_ref[pl.ds(e, 1), :]
    rank_row = rank_ref[pl.ds(e, 1), :]
    n_sel = jnp.max(rank_row, axis=1, keepdims=True)[0, 0].astype(jnp.int32) + 1

    def one_pass(j):
        slot = (lax.broadcasted_iota(jnp.int32, (rows, tm), 0) + j * rows).astype(F32)
        hit = rank_row == slot
        pick = jnp.where(hit, 1.0, 0.0).astype(BF16)
        hg = _dot(pick, h_ref[...]).astype(BF16)
        gu = _dot(hg, wgu_ref[...])
        gt, up = gu[:, :d_expert], gu[:, d_expert:]
        y = _mm1(gt * _sigmoid(gt) * up, wd_ref[...])
        gcol = jnp.sum(jnp.where(hit, g_row, 0.0), axis=1, keepdims=True)
        o_ref[...] += _dot(pick, (y * gcol).astype(BF16), TN)
        return j + 1

    lax.while_loop(lambda j: j * rows < n_sel, one_pass, jnp.int32(0))


def _moe(h2, logits_t, wgu_bf16, wd_bf16):
    n, d = h2.shape
    n_exp, _, two_de = wgu_bf16.shape
    tm = MOE_TM
    assert n % tm == 0
    once = pl.Buffered(1)
    return pl.pallas_call(
        functools.partial(_moe_kernel, d_expert=two_de // 2, rows=MOE_ROWS),
        grid=(n // tm, n_exp),
        in_specs=[pl.BlockSpec((tm, d), lambda i, e: (i, 0), pipeline_mode=once),
                  pl.BlockSpec((n_exp, tm), lambda i, e: (0, i)),
                  pl.BlockSpec((None, d, two_de), lambda i, e: (e, 0, 0)),
                  pl.BlockSpec((None, two_de // 2, d), lambda i, e: (e, 0, 0))],
        out_specs=pl.BlockSpec((tm, d), lambda i, e: (i, 0), pipeline_mode=once),
        out_shape=jax.ShapeDtypeStruct((n, d), F32),
        scratch_shapes=[pltpu.VMEM((n_exp, tm), F32), pltpu.VMEM((n_exp, tm), F32)],
        compiler_params=_params("parallel", "arbitrary"),
        name="moe",
    )(h2, logits_t, wgu_bf16, wd_bf16)


def _residual_kernel(x_ref, f_ref, g2_ref, fg_ref, o_ref, *, final_norm):
    xn = x_ref[...] + g2_ref[...] * f_ref[...]
    o_ref[...] = _rms(xn) * fg_ref[...] if final_norm else xn


def _residual(x_mid, f, mods, layer, final_g, lay, tile_off, final_norm):
    n, d = x_mid.shape
    tm = TM_TOKEN
    tok = pl.BlockSpec((tm, d), lambda i: (i, 0))
    return pl.pallas_call(
        functools.partial(_residual_kernel, final_norm=final_norm),
        grid=(n // tm,),
        in_specs=[tok, tok, _mod_spec(lay, tm, layer, 5, d, tile_off, 0), pl.BlockSpec((1, d), lambda i: (0, 0))],
        out_specs=tok,
        out_shape=jax.ShapeDtypeStruct((n, d), F32),
        compiler_params=_params("parallel"),
        name="residual",
    )(x_mid, f, mods, final_g.reshape(1, d))


def kernel(x, c, ctx, c_ctx, ada_w, ada_b, norm1_g, norm2_g, w_in, w_out, gmlp_norm_g, gmlp_ws, gmlp_b,
           hgrn_lb_logits, hgrn_norm_g, rwkv_mu, rwkv_w0, rwkv_w_up, rwkv_a0, rwkv_a_up, rwkv_g_up,
           rwkv_k_k, rwkv_k_a, rwkv_r_k, rwkv_gn_g, rwkv_gn_b, router_w, router_b, moe_w_gate_up,
           moe_w_down, final_norm_g):
    bsz, tl, d = x.shape
    tc = ctx.shape[1]
    depth = ada_w.shape[0]
    lay = _Layout(bsz, tc, tl)
    a_cols = 2 * gmlp_norm_g.shape[1]
    b_cols = 5 * hgrn_norm_g.shape[1]
    c_width = rwkv_a0.shape[1]
    assert lay.n_ctx % TM_MATMUL == 0 and tl % TM_MATMUL == 0 and tc % SCAN_CHUNK == 0

    mod_rows = -(-(bsz + 1) // 8) * 8
    c_rows = jnp.zeros((mod_rows, d), F32).at[:bsz].set(c).at[bsz].set(c_ctx)
    mods = _adaln(c_rows, ada_w, ada_b).reshape(depth, mod_rows, 6, 1, d)

    lbs = jnp.cumsum(jax.nn.softmax(hgrn_lb_logits.astype(F32), axis=1), axis=1)
    lbs = lbs - lbs[:, :1]
    w_in_bf16 = w_in.astype(BF16)
    w_out_bf16 = w_out.astype(BF16)
    wgu_bf16 = moe_w_gate_up.astype(BF16)
    wd_bf16 = moe_w_down.astype(BF16)
    router_wt = router_w.T
    router_b_col = router_b.reshape(N_EXPERTS, 1)

    x_all = jnp.concatenate([ctx.reshape(lay.n_ctx, d), x.reshape(lay.n_lat, d)], axis=0)
    ctx_tiles = lay.n_ctx // TM_TOKEN
    all_tiles = lay.n // TM_TOKEN

    for l in range(depth):
        last = l == depth - 1
        tile_off = ctx_tiles if last else 0
        n_tiles = all_tiles - tile_off
        w_l = w_in_bf16[l]
        pa = _inproj(x_all, norm1_g[l], mods, l, w_l[:, :a_cols], a_cols, lay)
        pb = _inproj(x_all, norm1_g[l], mods, l, w_l[:, a_cols:a_cols + b_cols], b_cols // 3, lay)
        pc = _inproj(x_all, norm1_g[l], mods, l, w_l[:, a_cols + b_cols:], (w_l.shape[1] - a_cols - b_cols) // 2, lay)

        ya = _gmlp(pa, gmlp_norm_g[l], gmlp_ws[l], gmlp_b[l], tile_off, n_tiles)
        hf, hb = _hgrn(pb, lbs[:, l], lay)

        w_lora = jnp.zeros((DECAY_LORA + AAA_LORA, 3 * c_width), F32)
        w_lora = w_lora.at[:DECAY_LORA, :c_width].set(rwkv_w_up[l, 0])
        w_lora = w_lora.at[:DECAY_LORA, c_width:2 * c_width].set(rwkv_w_up[l, 1])
        w_lora = w_lora.at[DECAY_LORA:, 2 * c_width:].set(rwkv_a_up[l])
        r, lwf, lwb, k2, v, kk, ab, gate, bonus = _rwkv_prep(
            pc, lay, rwkv_mu[l], w_lora, rwkv_g_up[l], rwkv_w0[l], rwkv_a0[l], rwkv_k_k[l], rwkv_k_a[l],
            rwkv_r_k[l])
        rf, rb = _rwkv_scan(r, lwf, lwb, k2, v, kk, ab, lay)

        x_mid, h2, logits_t = _outproj(
            x_all, ya, hf, hb, pb, hgrn_norm_g[l], rf, rb, bonus, gate, rwkv_gn_g[l], rwkv_gn_b[l],
            w_out_bf16[l], mods, l, norm2_g[l], router_wt, router_b_col, lay, tile_off, n_tiles)
        f = _moe(h2, logits_t, wgu_bf16[l], wd_bf16[l])
        x_all = _residual(x_mid, f, mods, l, final_norm_g, lay, tile_off, final_norm=last)
    return x_all.reshape(bsz, tl, d)
```

```python
import functools
import math

import jax
import jax.numpy as jnp
from jax import lax
from jax.experimental import pallas as pl
from jax.experimental.pallas import tpu as pltpu

F32 = jnp.float32
BF16 = jnp.bfloat16

GRID_W = 64
A_HEAD_DIM = 128
B_HEAD_DIM = 128
C_HEAD_DIM = 64
GMLP_CHUNK = 128
DECAY_LORA = 64
AAA_LORA = 64
GATE_LORA = 128
N_EXPERTS = 16
N_EXPERT_GROUPS = 4
NORM_EPS = 1e-6
RWKV_GN_EPS = 64e-5
DECAY_SCALE = math.exp(-0.5)

LANES = 128
MXU_TILE = 256
SCAN_CHUNK = 64
SCAN_BLOCK = 128
HGRN_SUB = 8
TM_MATMUL = 512
TM_TOKEN = 256
MOE_TM = 1024
MOE_ROWS = 160
VMEM_LIMIT = 56 * 1024 * 1024

NN = (((1,), (0,)), ((), ()))
NT = (((1,), (1,)), ((), ()))
TN = (((0,), (0,)), ((), ()))


def _dot(a, b, dims=NN):
    return lax.dot_general(a, b, dims, preferred_element_type=F32)


def _mm1(a, b, dims=NN):
    return _dot(a.astype(BF16), b.astype(BF16), dims)


def _split2(x):
    hi = x.astype(BF16)
    lo = (x - hi.astype(F32)).astype(BF16)
    return hi, lo


def _split3(x):
    hi = x.astype(BF16)
    r1 = x - hi.astype(F32)
    mid = r1.astype(BF16)
    lo = (r1 - mid.astype(F32)).astype(BF16)
    return hi, mid, lo


def _mm3(a, b, dims=NN):
    ah, al = _split2(a)
    bh, bl = _split2(b)
    return _dot(ah, bh, dims) + (_dot(al, bh, dims) + _dot(ah, bl, dims))


def _mm3_split_rhs(a, bh, bl):
    ah, al = _split2(a)
    return _dot(ah, bh) + (_dot(al, bh) + _dot(ah, bl))


def _mm_exact_lhs(a_bf16, b, dims=NN):
    h, m, l = _split3(b)
    return _dot(a_bf16, h, dims) + (_dot(a_bf16, m, dims) + _dot(a_bf16, l, dims))


def _head_sums(x, seg):
    hi, lo = _split2(x)
    w = seg.shape[0]
    return jnp.concatenate(
        [_dot(hi[:, g:g + w], seg) + _dot(lo[:, g:g + w], seg) for g in range(0, x.shape[1], w)], axis=1)


def _head_block_diag(head_dim, value):
    hi = lax.broadcasted_iota(jnp.int32, (MXU_TILE, MXU_TILE), 0) // head_dim
    hj = lax.broadcasted_iota(jnp.int32, (MXU_TILE, MXU_TILE), 1) // head_dim
    return jnp.where(hi == hj, value, 0.0).astype(BF16)


def _sigmoid(x):
    return jax.nn.sigmoid(x)


def _rms(x):
    return x * lax.rsqrt(jnp.mean(x * x, axis=-1, keepdims=True) + NORM_EPS)


def _params(*sem):
    return pltpu.CompilerParams(dimension_semantics=sem, vmem_limit_bytes=VMEM_LIMIT)


def _adaln_kernel(c_ref, w_ref, b_ref, o_ref):
    c = c_ref[...]
    o_ref[...] = _mm3(c * _sigmoid(c), w_ref[...]) + b_ref[...]


def _adaln(c_rows, ada_w, ada_b):
    depth, d, n = ada_w.shape
    rows = c_rows.shape[0]
    tn = 1024
    return pl.pallas_call(
        _adaln_kernel,
        grid=(depth, n // tn),
        in_specs=[
            pl.BlockSpec((rows, d), lambda l, j: (0, 0)),
            pl.BlockSpec((None, d, tn), lambda l, j: (l, 0, j)),
            pl.BlockSpec((None, 1, tn), lambda l, j: (l, 0, j)),
        ],
        out_specs=pl.BlockSpec((None, rows, tn), lambda l, j: (l, 0, j)),
        out_shape=jax.ShapeDtypeStruct((depth, rows, n), F32),
        compiler_params=_params("parallel", "parallel"),
        name="adaln",
    )(c_rows, ada_w, ada_b.reshape(depth, 1, n))


class _Layout:
    def __init__(self, bsz, tc, tl):
        self.bsz, self.tc, self.tl = bsz, tc, tl
        self.n_ctx = bsz * tc
        self.n_lat = bsz * tl
        self.n = self.n_ctx + self.n_lat

    def mod_row(self, tile, tm):
        ctx_tiles = self.n_ctx // tm
        return jnp.where(tile < ctx_tiles, self.bsz, (tile - ctx_tiles) // (self.tl // tm))

    def scan_block(self, b, step, reverse):
        ncc = self.tc // SCAN_BLOCK
        ncl = self.tl // SCAN_BLOCK
        if reverse:
            mc = ncc - 1 - step
            ml = ncl - 1 - (step - ncc)
        else:
            mc = step
            ml = step - ncc
        return jnp.where(step < ncc, b * ncc + mc, self.bsz * ncc + b * ncl + ml)


def _mod_spec(lay, tm, layer, part, d, tile_off, grid_pos):
    def index(*ids):
        return (layer, lay.mod_row(ids[grid_pos] + tile_off, tm), part, 0, 0)

    return pl.BlockSpec((None, None, None, 1, d), index)


def _inproj_kernel(x_ref, g_ref, sc_ref, sh_ref, w_ref, o_ref):
    h = _rms(x_ref[...]) * g_ref[...] * (1.0 + sc_ref[...]) + sh_ref[...]
    o_ref[...] = _mm1(h, w_ref[...])


def _inproj(x_all, norm_g, mods, layer, w_bf16, tn, lay):
    n, d = x_all.shape
    cols = w_bf16.shape[1]
    tm = TM_MATMUL
    return pl.pallas_call(
        _inproj_kernel,
        grid=(cols // tn, n // tm),
        in_specs=[
            pl.BlockSpec((tm, d), lambda j, i: (i, 0)),
            pl.BlockSpec((1, d), lambda j, i: (0, 0)),
            _mod_spec(lay, tm, layer, 1, d, 0, 1),
            _mod_spec(lay, tm, layer, 0, d, 0, 1),
            pl.BlockSpec((d, tn), lambda j, i: (0, j)),
        ],
        out_specs=pl.BlockSpec((tm, tn), lambda j, i: (i, j)),
        out_shape=jax.ShapeDtypeStruct((n, cols), F32),
        compiler_params=_params("parallel", "parallel"),
        name="inproj",
    )(x_all, norm_g.reshape(1, d), mods, mods, w_bf16)


def _gmlp_kernel(p_ref, g_ref, ws_ref, wbt_ref, o_ref, *, heads):
    x = p_ref[...]
    ge = 0.5 * x * (1.0 + lax.erf(x * (1.0 / math.sqrt(2.0))))
    width = heads * A_HEAD_DIM
    tm = x.shape[0]
    for h in range(heads):
        lo, hi = h * A_HEAD_DIM, (h + 1) * A_HEAD_DIM
        vn = _rms(ge[:, width + lo:width + hi]) * g_ref[:, lo:hi]
        for c in range(tm // GMLP_CHUNK):
            r0, r1 = c * GMLP_CHUNK, (c + 1) * GMLP_CHUNK
            s = _mm1(ws_ref[h], vn[r0:r1]) + wbt_ref[:, h:h + 1]
            o_ref[r0:r1, lo:hi] = ge[r0:r1, lo:hi] * s


def _gmlp(pa, norm_g, ws, wb, tile_off, n_tiles):
    n, cols = pa.shape
    heads = ws.shape[0]
    width = cols // 2
    tm = TM_TOKEN
    return pl.pallas_call(
        functools.partial(_gmlp_kernel, heads=heads),
        grid=(n_tiles,),
        in_specs=[
            pl.BlockSpec((tm, cols), lambda i: (i + tile_off, 0)),
            pl.BlockSpec((1, width), lambda i: (0, 0)),
            pl.BlockSpec((heads, GMLP_CHUNK, GMLP_CHUNK), lambda i: (0, 0, 0)),
            pl.BlockSpec((GMLP_CHUNK, heads), lambda i: (0, 0)),
        ],
        out_specs=pl.BlockSpec((tm, width), lambda i: (i, 0)),
        out_shape=jax.ShapeDtypeStruct((n_tiles * tm, width), F32),
        compiler_params=_params("parallel"),
        name="gmlp",
    )(pa, norm_g.reshape(1, width), ws.astype(BF16), wb.T)


def _hgrn_kernel(qf_ref, if_ref, zf_ref, qb_ref, ib_ref, zb_ref, lb_ref, yf_ref, yb_ref, st_ref, *, heads):
    @pl.when(pl.program_id(1) == 0)
    def _():
        st_ref[...] = jnp.zeros_like(st_ref)

    cl, sub, hd = SCAN_CHUNK, HGRN_SUB, B_HEAD_DIM
    n_sub = qf_ref.shape[0] // cl
    row = lax.broadcasted_iota(jnp.int32, (cl, cl), 0)
    col = lax.broadcasted_iota(jnp.int32, (cl, cl), 1)

    def chunk(step, state):
        dirs = []
        for z, (q_ref, i_ref, z_ref) in enumerate(((qf_ref, if_ref, zf_ref), (qb_ref, ib_ref, zb_ref))):
            reverse = z == 1
            first = (n_sub - 1 - step if reverse else step) * cl
            rows = slice(first, first + cl)
            q = q_ref[rows, :]
            q = q * _sigmoid(q)
            v = i_ref[rows, :]
            zz = z_ref[rows, :]
            lb = lb_ref[z:z + 1]
            pos = zz > 0.0
            t = jnp.exp(-jnp.abs(zz))
            den = 1.0 + t
            k = (1.0 - lb) * jnp.where(pos, t, 1.0) / den
            num = jnp.where(pos, 1.0 + lb * t, lb + t)
            ok = num > 0.0
            lf = jnp.where(ok, jnp.log(jnp.where(ok, num, 1.0)), zz) - jnp.log(den)

            ut = (cl - 1 - row) if reverse else row
            us = (cl - 1 - col) if reverse else col
            sel = [us <= ut]
            keeps = []
            half = cl // 2
            while half >= sub:
                mid = (ut // (2 * half)) * (2 * half) + half
                sel.append(us < mid)
                keeps.append((ut >= mid) & (us < mid) & (us >= mid - half))
                half //= 2
            start = (ut // sub) * sub
            sel.append(us < start)
            keeps.append((us >= start) & (us <= ut))
            cums = _mm_exact_lhs(jnp.concatenate(sel, axis=0).astype(BF16), lf)
            c = cums[0:cl]
            total = c[0:1] if reverse else c[cl - 1:cl]
            dirs.append(dict(q=q, k=k, v=v, c=c, cums=cums, total=total, keeps=keeps, rows=rows,
                             qe=q * jnp.exp(c), ke=k * jnp.exp(total - c), scores=[None] * heads))

        for lvl in range(len(dirs[0]["keeps"])):
            for d in dirs:
                ref = d["cums"][(lvl + 1) * cl:(lvl + 2) * cl]
                dq, dk = d["c"] - ref, ref - d["c"]
                if lvl + 1 < len(d["keeps"]):
                    dq, dk = jnp.minimum(dq, 0.0), jnp.minimum(dk, 0.0)
                qs = d["q"] * jnp.exp(dq)
                ks = d["k"] * jnp.exp(dk)
                for h in range(heads):
                    lo, hi = h * hd, (h + 1) * hd
                    s = jnp.where(d["keeps"][lvl], _mm1(qs[:, lo:hi], ks[:, lo:hi], NT), 0.0)
                    d["scores"][h] = s if d["scores"][h] is None else d["scores"][h] + s

        new_state = []
        for z, (d, y_ref) in enumerate(zip(dirs, (yf_ref, yb_ref))):
            for h in range(heads):
                lo, hi = h * hd, (h + 1) * hd
                st = state[z * heads + h]
                y_ref[d["rows"], lo:hi] = (_mm1(d["qe"][:, lo:hi], st, NT)
                                           + _mm1(d["scores"][h], d["v"][:, lo:hi]))
                new_state.append(st * jnp.exp(d["total"][:, lo:hi])
                                 + _mm1(d["v"][:, lo:hi], d["ke"][:, lo:hi], TN))
        return new_state

    state = [st_ref[z, h] for z in range(2) for h in range(heads)]
    for step in range(n_sub):
        state = chunk(step, state)
    for i, s in enumerate(state):
        st_ref[i // heads, i % heads] = s


def _hgrn(pb, lb_dirs, lay):
    n = pb.shape[0]
    width = pb.shape[1] // 5
    heads = width // B_HEAD_DIM
    steps = (lay.tc + lay.tl) // SCAN_BLOCK
    blk = lambda comp, reverse: pl.BlockSpec(
        (SCAN_BLOCK, width), lambda b, s: (lay.scan_block(b, s, reverse), comp))
    out = jax.ShapeDtypeStruct((n, width), F32)
    return pl.pallas_call(
        functools.partial(_hgrn_kernel, heads=heads),
        grid=(lay.bsz, steps),
        in_specs=[blk(0, False), blk(1, False), blk(2, False), blk(0, True), blk(1, True), blk(3, True),
                  pl.BlockSpec((2, width), lambda b, s: (0, 0))],
        out_specs=[blk(0, False), blk(0, True)],
        out_shape=[out, out],
        scratch_shapes=[pltpu.VMEM((2, heads, B_HEAD_DIM, B_HEAD_DIM), F32)],
        compiler_params=_params("parallel", "arbitrary"),
        name="hgrn",
    )(pb, pb, pb, pb, pb, pb, lb_dirs)


def _rwkv_prep_kernel(cur_ref, prev_ref, next_ref, mu_ref, wlh_ref, wll_ref, guph_ref, gupl_ref, w0f_ref,
                      w0b_ref, a0_ref, kk_ref, ka_ref, rk_ref, seg_ref,
                      r_out, lwf_out, lwb_out, k_out, v_out, kk_out, ab_out, g_out, bonus_out,
                      *, ctx_tiles, rows_per_batch, width):
    tile = pl.program_id(0)
    z = cur_ref[...]
    tm, cols = z.shape
    halo = prev_ref.shape[0]
    ext = jnp.concatenate([prev_ref[...], z, next_ref[...]], axis=0)
    left = ext[halo - 1:halo - 1 + tm]
    right = ext[halo + 1:halo + 1 + tm]
    up = ext[0:tm]
    down = ext[2 * halo:2 * halo + tm]

    is_ctx = tile < ctx_tiles
    t = lax.broadcasted_iota(jnp.int32, (tm, 1), 0)
    rowb = t + ((tile - ctx_tiles) * tm) % rows_per_batch
    line = jnp.where(is_ctx, tm, GRID_W)
    pos = t & (line - 1)
    left = jnp.where(pos != 0, left, 0.0)
    right = jnp.where(pos != line - 1, right, 0.0)
    up = jnp.where(rowb >= GRID_W, up, 0.0)
    down = jnp.where(rowb < rows_per_batch - GRID_W, down, 0.0)
    slot = lax.broadcasted_iota(jnp.int32, (1, cols), 1) % 4
    slot = jnp.where(is_ctx, slot % 2, slot)
    shifted = jnp.where(slot == 0, left, jnp.where(slot == 1, right, jnp.where(slot == 2, up, down)))
    xx = z + mu_ref[...] * (shifted - z)

    r = xx[:, 0:width]
    k = xx[:, width:2 * width]
    v = xx[:, 2 * width:3 * width]
    lora = xx[:, 3 * width:3 * width + DECAY_LORA + AAA_LORA]
    lane = lax.broadcasted_iota(jnp.int32, (1, DECAY_LORA + AAA_LORA), 1)
    lora = jnp.where(lane < DECAY_LORA, jnp.tanh(lora), lora)
    up3 = _mm3_split_rhs(lora, wlh_ref[...], wll_ref[...])
    gd = xx[:, 3 * width + DECAY_LORA + AAA_LORA:]
    g_out[...] = _mm3_split_rhs(_sigmoid(gd), guph_ref[...], gupl_ref[...])
    lwf_out[...] = -DECAY_SCALE * _sigmoid(w0f_ref[...] + up3[:, 0:width])
    lwb_out[...] = -DECAY_SCALE * _sigmoid(w0b_ref[...] + up3[:, width:2 * width])
    a = _sigmoid(a0_ref[...] + up3[:, 2 * width:3 * width])

    seg = seg_ref[...]
    kk = k * kk_ref[...]
    kk = kk / jnp.maximum(jnp.sqrt(_head_sums(kk * kk, seg)), 1e-12)
    k2 = k * (1.0 + (a - 1.0) * ka_ref[...])
    r_out[...] = r
    k_out[...] = k2
    v_out[...] = v
    kk_out[...] = kk
    ab_out[...] = kk * a
    bonus_out[...] = _head_sums(r * k2 * rk_ref[...], seg) * v


def _rwkv_prep(pc, lay, mu, w_lora, g_up, w0, a0, k_k, k_a, r_k):
    n, cols = pc.shape
    width = a0.shape[-1]
    tm = TM_TOKEN
    assert lay.tc == tm and lay.tl % tm == 0 and tm % GRID_W == 0
    halo = GRID_W
    hb = tm // halo
    last = n // halo - 1
    row = lambda a: a.reshape(1, -1)
    seg = _head_block_diag(C_HEAD_DIM, 1.0)
    assert width % MXU_TILE == 0
    full = lambda a: pl.BlockSpec(a.shape, lambda i: (0,) * a.ndim)
    consts = [row(mu), *_split2(w_lora), *_split2(g_up), row(w0[0]), row(w0[1]), row(a0), row(k_k), row(k_a),
              row(r_k), seg]
    out_spec = pl.BlockSpec((tm, width), lambda i: (i, 0))
    return pl.pallas_call(
        functools.partial(_rwkv_prep_kernel, ctx_tiles=lay.n_ctx // tm, rows_per_batch=lay.tl, width=width),
        grid=(n // tm,),
        in_specs=[
            pl.BlockSpec((tm, cols), lambda i: (i, 0)),
            pl.BlockSpec((halo, cols), lambda i: (jnp.maximum(i * hb - 1, 0), 0)),
            pl.BlockSpec((halo, cols), lambda i: (jnp.minimum((i + 1) * hb, last), 0)),
        ] + [full(a) for a in consts],
        out_specs=[out_spec] * 9,
        out_shape=[jax.ShapeDtypeStruct((n, width), F32)] * 9,
        compiler_params=_params("parallel"),
        name="rwkv_prep",
    )(pc, pc, pc, *consts)


def _rwkv_kernel(rf_ref, kf_ref, vf_ref, kkf_ref, abf_ref, lwf_ref, rb_ref, kb_ref, vb_ref, kkb_ref, abb_ref,
                 lwb_ref, of_ref, ob_ref, st_ref, *, pairs):
    @pl.when(pl.program_id(1) == 0)
    def _():
        st_ref[...] = jnp.zeros_like(st_ref)

    cl, hd, pw = SCAN_CHUNK, C_HEAD_DIM, 2 * C_HEAD_DIM
    n_sub = rf_ref.shape[0] // cl
    cat = lambda a, b: jnp.concatenate([a, b], axis=0)
    cut = lambda x: [x[:, p * pw:(p + 1) * pw] for p in range(pairs)]
    row = lax.broadcasted_iota(jnp.int32, (cl, cl), 0)
    col = lax.broadcasted_iota(jnp.int32, (cl, cl), 1)
    lane = lax.broadcasted_iota(jnp.int32, (cl, pw), 1)
    m1 = (lane < hd).astype(F32)
    m2 = 1.0 - m1
    r2 = lax.broadcasted_iota(jnp.int32, (2 * cl, 2 * cl), 0)
    c2 = lax.broadcasted_iota(jnp.int32, (2 * cl, 2 * cl), 1)
    tt, ss = r2 % cl, c2 % cl
    same = (r2 // cl) == (c2 // cl)
    eye = jnp.where(r2 == c2, 1.0, 0.0)
    ti = lax.broadcasted_iota(jnp.int32, (cl, 2 * cl), 0)
    si = lax.broadcasted_iota(jnp.int32, (cl, 2 * cl), 1) % cl
    every = range(2 * pairs)

    def chunk(sub, s0):
        kkg, rg, kd, bd, kc, bc, vs, decay, strict, incl, rows = ([] for _ in range(11))
        for z, refs in enumerate(((rf_ref, kf_ref, vf_ref, kkf_ref, abf_ref, lwf_ref),
                                  (rb_ref, kb_ref, vb_ref, kkb_ref, abb_ref, lwb_ref))):
            reverse = z == 1
            first = (n_sub - 1 - sub if reverse else sub) * cl
            rows.append(slice(first, first + cl))
            r, k, v, kk, ab, lw = (ref[rows[z], :] for ref in refs)
            tri = (col >= row) if reverse else (col <= row)
            c = _mm_exact_lhs(tri.astype(BF16), lw)
            total = c[0:1] if reverse else c[cl - 1:cl]
            e_out = jnp.exp(-c)
            e_last = jnp.exp(total - c)
            kkg += cut(kk * jnp.exp(c - lw))
            rg += cut(r * jnp.exp(c))
            kd += cut(k * e_out)
            bd += cut(ab * e_out)
            kc += cut(k * e_last)
            bc += cut(ab * e_last)
            vs += cut(v)
            decay += cut(jnp.exp(total))
            strict += [(ss > tt) if reverse else (ss < tt)] * pairs
            incl += [(si >= ti) if reverse else (si <= ti)] * pairs

        g1 = [_mm1(cat(kkg[p] * m1, rg[p] * m1), cat(bd[p], kd[p]), NT) for p in every]
        g2 = [_mm1(cat(kkg[p] * m2, rg[p] * m2), cat(kd[p], bd[p]), NT) for p in every]
        xs = [_mm1(cat(kkg[p], rg[p]), s0[p], NT) for p in every]
        gl = [jnp.where(strict[p], cat(g1[p][0:cl], g2[p][0:cl]), 0.0) for p in every]
        l_diag = [jnp.where(same, gl[p], 0.0) for p in every]
        l_anti = [jnp.where(same, 0.0, gl[p]) for p in every]
        lkv = [_mm1(l_anti[p], cat(vs[p] * m2, vs[p] * m1)) for p in every]

        inv = [eye - l_diag[p] for p in every]
        pwr = [_mm1(l_diag[p], l_diag[p]) for p in every]
        n_sq = int(math.log2(cl)) - 1
        for i in range(n_sq):
            inv = [inv[p] + _mm1(inv[p], pwr[p]) for p in every]
            if i + 1 < n_sq:
                pwr = [_mm1(pwr[p], pwr[p]) for p in every]

        rhs = [xs[p][0:cl] + lkv[p][0:cl] + lkv[p][cl:] for p in every]
        ust = [_mm1(inv[p], cat(rhs[p] * m1, rhs[p] * m2)) for p in every]
        u = [ust[p][0:cl] + ust[p][cl:] for p in every]
        o1 = [_mm1(jnp.where(incl[p], g1[p][cl:], 0.0), cat(-u[p] * m1, vs[p] * m1)) for p in every]
        o2 = [_mm1(jnp.where(incl[p], g2[p][cl:], 0.0), cat(vs[p] * m2, -u[p] * m2)) for p in every]
        upd = [_mm1(cat(vs[p], -u[p]), cat(kc[p], bc[p]), TN) for p in every]
        for p in every:
            z, q = divmod(p, pairs)
            o_ref = ob_ref if z else of_ref
            o_ref[rows[z], q * pw:(q + 1) * pw] = xs[p][cl:] + o1[p] + o2[p]
        return [s0[p] * decay[p] + jnp.where(same, upd[p], 0.0) for p in every]

    state = [st_ref[p // pairs, p % pairs] for p in every]
    for sub in range(n_sub):
        state = chunk(sub, state)
    for p in every:
        st_ref[p // pairs, p % pairs] = state[p]


def _rwkv_scan(r, lwf, lwb, k, v, kk, ab, lay):
    n, width = r.shape
    pairs = width // (2 * C_HEAD_DIM)
    steps = (lay.tc + lay.tl) // SCAN_BLOCK
    blk = lambda reverse: pl.BlockSpec(
        (SCAN_BLOCK, width), lambda b, s: (lay.scan_block(b, s, reverse), 0))
    out = jax.ShapeDtypeStruct((n, width), F32)
    return pl.pallas_call(
        functools.partial(_rwkv_kernel, pairs=pairs),
        grid=(lay.bsz, steps),
        in_specs=[blk(False)] * 6 + [blk(True)] * 6,
        out_specs=[blk(False), blk(True)],
        out_shape=[out, out],
        scratch_shapes=[pltpu.VMEM((2, pairs, 2 * C_HEAD_DIM, 2 * C_HEAD_DIM), F32)],
        compiler_params=_params("parallel", "arbitrary"),
        name="rwkv",
    )(r, k, v, kk, ab, lwf, r, k, v, kk, ab, lwb)


def _outproj_kernel(x_ref, ya_ref, hf_ref, hb_ref, og_ref, hg_ref, rf_ref, rb_ref, bonus_ref, gate_ref,
                    gng_ref, gnb_ref, avg_ref, w_ref, g1_ref, n2_ref, sc2_ref, sh2_ref, rw_ref, rbias_ref,
                    xo_ref, h2_ref, lg_ref, *, a_width, b_width):
    o = hf_ref[...] + hb_ref[...]
    og = og_ref[...]
    yb = []
    for h in range(b_width // B_HEAD_DIM):
        lo, hi = h * B_HEAD_DIM, (h + 1) * B_HEAD_DIM
        yb.append(_rms(o[:, lo:hi]) * hg_ref[:, lo:hi] * _sigmoid(og[:, lo:hi]))
    yb = jnp.concatenate(yb, axis=1)

    rw = rf_ref[...] + rb_ref[...]
    avg = avg_ref[...]
    xc = rw - _head_sums(rw, avg)
    var = _head_sums(xc * xc, avg)
    yc = (xc * lax.rsqrt(var + RWKV_GN_EPS) * gng_ref[...] + gnb_ref[...] + bonus_ref[...]) * gate_ref[...]

    y = (_mm1(ya_ref[...], w_ref[0:a_width]) + _mm1(yb, w_ref[a_width:a_width + b_width])
         + _mm1(yc, w_ref[a_width + b_width:]))
    xn = x_ref[...] + g1_ref[...] * y
    xo_ref[...] = xn
    h2 = _rms(xn) * n2_ref[...] * (1.0 + sc2_ref[...]) + sh2_ref[...]
    h2_ref[...] = h2.astype(BF16)
    lg_ref[...] = _mm3(rw_ref[...], h2, NT) + rbias_ref[...]


def _outproj(x_all, ya, hf, hb, pb, hgrn_g, rf, rb, bonus, gate, gn_g, gn_b, w_out_bf16, mods, layer,
             norm2_g, router_wt, router_b_col, lay, tile_off, n_tiles):
    n, d = x_all.shape
    a_width = ya.shape[1]
    b_width = hf.shape[1]
    c_width = rf.shape[1]
    tm = TM_TOKEN
    avg = _head_block_diag(C_HEAD_DIM, 1.0 / C_HEAD_DIM)
    assert c_width % MXU_TILE == 0
    tok = lambda w, comp=0, off=tile_off: pl.BlockSpec((tm, w), lambda i: (i + off, comp))
    full = lambda a: pl.BlockSpec(a.shape, lambda i: (0,) * a.ndim)
    row = lambda a: a.reshape(1, -1)
    mod = lambda part: _mod_spec(lay, tm, layer, part, d, tile_off, 0)
    hg, gg, gb, n2 = row(hgrn_g), row(gn_g), row(gn_b), row(norm2_g)
    rows = n_tiles * tm
    return pl.pallas_call(
        functools.partial(_outproj_kernel, a_width=a_width, b_width=b_width),
        grid=(n_tiles,),
        in_specs=[tok(d), tok(a_width, off=0), tok(b_width), tok(b_width), tok(b_width, 4), full(hg),
                  tok(c_width), tok(c_width), tok(c_width), tok(c_width), full(gg), full(gb), full(avg),
                  full(w_out_bf16), mod(2), full(n2), mod(4), mod(3), full(router_wt), full(router_b_col)],
        out_specs=[tok(d, off=0), tok(d, off=0), pl.BlockSpec((N_EXPERTS, tm), lambda i: (0, i))],
        out_shape=[jax.ShapeDtypeStruct((rows, d), F32), jax.ShapeDtypeStruct((rows, d), BF16),
                   jax.ShapeDtypeStruct((N_EXPERTS, rows), F32)],
        compiler_params=_params("parallel"),
        name="outproj",
    )(x_all, ya, hf, hb, pb, hg, rf, rb, bonus, gate, gg, gb, avg, w_out_bf16, mods, n2, mods, mods,
      router_wt, router_b_col)


def _route(logits):
    per_group = N_EXPERTS // N_EXPERT_GROUPS
    assert per_group == 4
    expert = lax.broadcasted_iota(jnp.int32, logits.shape, 0)
    ex = jnp.exp(logits - jnp.max(logits, axis=0, keepdims=True))
    p = ex / jnp.sum(ex, axis=0, keepdims=True)

    def top2_sum(a, b, c, d):
        h1, l1, h2, l2 = jnp.maximum(a, b), jnp.minimum(a, b), jnp.maximum(c, d), jnp.minimum(c, d)
        return jnp.maximum(h1, h2) + jnp.maximum(jnp.minimum(h1, h2), jnp.maximum(l1, l2))

    score = [top2_sum(*(p[g * per_group + i:g * per_group + i + 1] for i in range(per_group)))
             for g in range(N_EXPERT_GROUPS)]
    best, sel = score[0], jnp.zeros(score[0].shape, jnp.int32)
    for g in range(1, N_EXPERT_GROUPS):
        better = score[g] > best
        best = jnp.where(better, score[g], best)
        sel = jnp.where(better, g, sel)
    pm = jnp.where(expert // per_group == sel, p, -1.0)

    def first_max(vals):
        top = jnp.max(vals, axis=0, keepdims=True)
        return top, jnp.min(jnp.where(vals == top, expert, N_EXPERTS), axis=0, keepdims=True)

    p1, i1 = first_max(pm)
    p2, i2 = first_max(jnp.where(expert == i1, -2.0, pm))
    tot = p1 + p2
    gates = jnp.where(expert == i1, p1 / tot, 0.0) + jnp.where(expert == i2, p2 / tot, 0.0)
    return gates, (expert == i1) | (expert == i2)


def _moe_kernel(h_ref, lg_ref, x_ref, g2_ref, fg_ref, wgu_ref, wd_ref, o_ref, gate_ref, rank_ref,
                *, d_expert, rows, final_norm):
    e = pl.program_id(1)
    tm = h_ref.shape[0]

    @pl.when(e == 0)
    def _():
        o_ref[...] = x_ref[...]
        gates, chosen = _route(lg_ref[...])
        lane = lax.broadcasted_iota(jnp.int32, chosen.shape, 1)
        count = jnp.where(chosen, 1.0, 0.0)
        step = 1
        while step < tm:
            count = count + jnp.where(lane >= step, pltpu.roll(count, step, axis=1), 0.0)
            step *= 2
        gate_ref[...] = gates
        rank_ref[...] = jnp.where(chosen, count - 1.0, -1.0)

    g_row = gate_ref[pl.ds(e, 1), :]
    rank_row = rank_ref[pl.ds(e, 1), :]
    n_sel = jnp.max(rank_row, axis=1, keepdims=True)[0, 0].astype(jnp.int32) + 1

    def one_pass(j):
        slot = (lax.broadcasted_iota(jnp.int32, (rows, tm), 0) + j * rows).astype(F32)
        hit = rank_row == slot
        pick = jnp.where(hit, 1.0, 0.0).astype(BF16)
        hg = _dot(pick, h_ref[...]).astype(BF16)
        gu = _dot(hg, wgu_ref[...])
        gt, up = gu[:, :d_expert], gu[:, d_expert:]
        y = _mm1(gt * _sigmoid(gt) * up, wd_ref[...])
        gcol = jnp.sum(jnp.where(hit, g_row, 0.0), axis=1, keepdims=True)
        o_ref[...] += _dot(pick, (y * gcol * g2_ref[...]).astype(BF16), TN)
        return j + 1

    lax.while_loop(lambda j: j * rows < n_sel, one_pass, jnp.int32(0))

    if final_norm:
        @pl.when(e == pl.num_programs(1) - 1)
        def _():
            o_ref[...] = _rms(o_ref[...]) * fg_ref[...]


def _moe(h2, logits_t, x_mid, mods, layer, wgu_bf16, wd_bf16, final_g, lay, row_off, final_norm):
    n, d = h2.shape
    n_exp, _, two_de = wgu_bf16.shape
    tm = MOE_TM
    assert n % tm == 0 and row_off % tm == 0 and lay.n_ctx % tm == 0 and lay.tl % tm == 0
    once = pl.Buffered(1)
    tok = pl.BlockSpec((tm, d), lambda i, e: (i, 0), pipeline_mode=once)
    return pl.pallas_call(
        functools.partial(_moe_kernel, d_expert=two_de // 2, rows=MOE_ROWS, final_norm=final_norm),
        grid=(n // tm, n_exp),
        in_specs=[tok, pl.BlockSpec((n_exp, tm), lambda i, e: (0, i)), tok,
                  _mod_spec(lay, tm, layer, 5, d, row_off // tm, 0),
                  pl.BlockSpec((1, d), lambda i, e: (0, 0)),
                  pl.BlockSpec((None, d, two_de), lambda i, e: (e, 0, 0)),
                  pl.BlockSpec((None, two_de // 2, d), lambda i, e: (e, 0, 0))],
        out_specs=tok,
        out_shape=jax.ShapeDtypeStruct((n, d), F32),
        scratch_shapes=[pltpu.VMEM((n_exp, tm), F32), pltpu.VMEM((n_exp, tm), F32)],
        compiler_params=_params("parallel", "arbitrary"),
        name="moe",
    )(h2, logits_t, x_mid, mods, final_g.reshape(1, d), wgu_bf16, wd_bf16)


def kernel(x, c, ctx, c_ctx, ada_w, ada_b, norm1_g, norm2_g, w_in, w_out, gmlp_norm_g, gmlp_ws, gmlp_b,
           hgrn_lb_logits, hgrn_norm_g, rwkv_mu, rwkv_w0, rwkv_w_up, rwkv_a0, rwkv_a_up, rwkv_g_up,
           rwkv_k_k, rwkv_k_a, rwkv_r_k, rwkv_gn_g, rwkv_gn_b, router_w, router_b, moe_w_gate_up,
           moe_w_down, final_norm_g):
    bsz, tl, d = x.shape
    tc = ctx.shape[1]
    depth = ada_w.shape[0]
    lay = _Layout(bsz, tc, tl)
    a_cols = 2 * gmlp_norm_g.shape[1]
    b_cols = 5 * hgrn_norm_g.shape[1]
    c_width = rwkv_a0.shape[1]
    assert lay.n_ctx % TM_MATMUL == 0 and tl % TM_MATMUL == 0 and tc % SCAN_BLOCK == 0

    mod_rows = -(-(bsz + 1) // 8) * 8
    c_rows = jnp.zeros((mod_rows, d), F32).at[:bsz].set(c).at[bsz].set(c_ctx)
    mods = _adaln(c_rows, ada_w, ada_b).reshape(depth, mod_rows, 6, 1, d)

    lbs = jnp.cumsum(jax.nn.softmax(hgrn_lb_logits.astype(F32), axis=1), axis=1)
    lbs = lbs - lbs[:, :1]
    w_in_bf16 = w_in.astype(BF16)
    w_out_bf16 = w_out.astype(BF16)
    wgu_bf16 = moe_w_gate_up.astype(BF16)
    wd_bf16 = moe_w_down.astype(BF16)
    router_wt = router_w.T
    router_b_col = router_b.reshape(N_EXPERTS, 1)

    x_all = jnp.concatenate([ctx.reshape(lay.n_ctx, d), x.reshape(lay.n_lat, d)], axis=0)
    ctx_tiles = lay.n_ctx // TM_TOKEN
    all_tiles = lay.n // TM_TOKEN

    for l in range(depth):
        last = l == depth - 1
        tile_off = ctx_tiles if last else 0
        n_tiles = all_tiles - tile_off
        w_l = w_in_bf16[l]
        pa = _inproj(x_all, norm1_g[l], mods, l, w_l[:, :a_cols], a_cols, lay)
        pb = _inproj(x_all, norm1_g[l], mods, l, w_l[:, a_cols:a_cols + b_cols], b_cols // 3, lay)
        pc = _inproj(x_all, norm1_g[l], mods, l, w_l[:, a_cols + b_cols:], (w_l.shape[1] - a_cols - b_cols) // 2, lay)

        ya = _gmlp(pa, gmlp_norm_g[l], gmlp_ws[l], gmlp_b[l], tile_off, n_tiles)
        hf, hb = _hgrn(pb, lbs[:, l], lay)

        w_lora = jnp.zeros((DECAY_LORA + AAA_LORA, 3 * c_width), F32)
        w_lora = w_lora.at[:DECAY_LORA, :c_width].set(rwkv_w_up[l, 0])
        w_lora = w_lora.at[:DECAY_LORA, c_width:2 * c_width].set(rwkv_w_up[l, 1])
        w_lora = w_lora.at[DECAY_LORA:, 2 * c_width:].set(rwkv_a_up[l])
        r, lwf, lwb, k2, v, kk, ab, gate, bonus = _rwkv_prep(
            pc, lay, rwkv_mu[l], w_lora, rwkv_g_up[l], rwkv_w0[l], rwkv_a0[l], rwkv_k_k[l], rwkv_k_a[l],
            rwkv_r_k[l])
        rf, rb = _rwkv_scan(r, lwf, lwb, k2, v, kk, ab, lay)

        x_mid, h2, logits_t = _outproj(
            x_all, ya, hf, hb, pb, hgrn_norm_g[l], rf, rb, bonus, gate, rwkv_gn_g[l], rwkv_gn_b[l],
            w_out_bf16[l], mods, l, norm2_g[l], router_wt, router_b_col, lay, tile_off, n_tiles)
        x_all = _moe(h2, logits_t, x_mid, mods, l, wgu_bf16[l], wd_bf16[l], final_norm_g, lay,
                     tile_off * TM_TOKEN, final_norm=last)
    return x_all.reshape(bsz, tl, d)
```

```python
import functools
import math

import jax
import jax.numpy as jnp
from jax import lax
from jax.experimental import pallas as pl
from jax.experimental.pallas import tpu as pltpu

F32 = jnp.float32
BF16 = jnp.bfloat16

GRID_W = 64
A_HEAD_DIM = 128
B_HEAD_DIM = 128
C_HEAD_DIM = 64
GMLP_CHUNK = 128
DECAY_LORA = 64
AAA_LORA = 64
GATE_LORA = 128
N_EXPERTS = 16
N_EXPERT_GROUPS = 4
NORM_EPS = 1e-6
RWKV_GN_EPS = 64e-5
DECAY_SCALE = math.exp(-0.5)

LANES = 128
MXU_TILE = 256
SCAN_CHUNK = 64
SCAN_BLOCK = 128
HGRN_SUB = 8
TM_MATMUL = 512
TM_TOKEN = 256
ROW_BLOCK = 128
MOE_TM = 1024
MOE_ROWS = 160
VMEM_LIMIT = 56 * 1024 * 1024

NN = (((1,), (0,)), ((), ()))
NT = (((1,), (1,)), ((), ()))
TN = (((0,), (0,)), ((), ()))


def _dot(a, b, dims=NN):
    return lax.dot_general(a, b, dims, preferred_element_type=F32)


def _mm1(a, b, dims=NN):
    return _dot(a.astype(BF16), b.astype(BF16), dims)


def _split2(x):
    hi = x.astype(BF16)
    lo = (x - hi.astype(F32)).astype(BF16)
    return hi, lo


def _split3(x):
    hi = x.astype(BF16)
    r1 = x - hi.astype(F32)
    mid = r1.astype(BF16)
    lo = (r1 - mid.astype(F32)).astype(BF16)
    return hi, mid, lo


def _mm3(a, b, dims=NN):
    ah, al = _split2(a)
    bh, bl = _split2(b)
    return _dot(ah, bh, dims) + (_dot(al, bh, dims) + _dot(ah, bl, dims))


def _mm3_split_rhs(a, bh, bl):
    ah, al = _split2(a)
    return _dot(ah, bh) + (_dot(al, bh) + _dot(ah, bl))


def _mm_exact_lhs(a_bf16, b, dims=NN):
    h, m, l = _split3(b)
    return _dot(a_bf16, h, dims) + (_dot(a_bf16, m, dims) + _dot(a_bf16, l, dims))


def _head_sums(x, seg):
    hi, lo = _split2(x)
    w = seg.shape[0]
    return jnp.concatenate(
        [_dot(hi[:, g:g + w], seg) + _dot(lo[:, g:g + w], seg) for g in range(0, x.shape[1], w)], axis=1)


def _head_block_diag(head_dim, value):
    hi = lax.broadcasted_iota(jnp.int32, (MXU_TILE, MXU_TILE), 0) // head_dim
    hj = lax.broadcasted_iota(jnp.int32, (MXU_TILE, MXU_TILE), 1) // head_dim
    return jnp.where(hi == hj, value, 0.0).astype(BF16)


def _sigmoid(x):
    return jax.nn.sigmoid(x)


def _rms(x):
    return x * lax.rsqrt(jnp.mean(x * x, axis=-1, keepdims=True) + NORM_EPS)


def _params(*sem):
    return pltpu.CompilerParams(dimension_semantics=sem, vmem_limit_bytes=VMEM_LIMIT)


def _adaln_kernel(c_ref, w_ref, b_ref, o_ref):
    c = c_ref[...]
    o_ref[...] = _mm3(c * _sigmoid(c), w_ref[...]) + b_ref[...]


def _adaln(c_rows, ada_w, ada_b):
    depth, d, n = ada_w.shape
    rows = c_rows.shape[0]
    tn = 1024
    return pl.pallas_call(
        _adaln_kernel,
        grid=(depth, n // tn),
        in_specs=[
            pl.BlockSpec((rows, d), lambda l, j: (0, 0)),
            pl.BlockSpec((None, d, tn), lambda l, j: (l, 0, j)),
            pl.BlockSpec((None, 1, tn), lambda l, j: (l, 0, j)),
        ],
        out_specs=pl.BlockSpec((None, rows, tn), lambda l, j: (l, 0, j)),
        out_shape=jax.ShapeDtypeStruct((depth, rows, n), F32),
        compiler_params=_params("parallel", "parallel"),
        name="adaln",
    )(c_rows, ada_w, ada_b.reshape(depth, 1, n))


class _Layout:
    def __init__(self, bsz, tc, tl):
        self.bsz, self.tc, self.tl = bsz, tc, tl
        self.n_ctx = bsz * tc
        self.n_lat = bsz * tl
        self.n = self.n_ctx + self.n_lat

    def mod_row(self, tile, tm):
        ctx_tiles = self.n_ctx // tm
        return jnp.where(tile < ctx_tiles, self.bsz, (tile - ctx_tiles) // (self.tl // tm))

    def scan_block(self, b, step, reverse):
        ncc = self.tc // SCAN_BLOCK
        ncl = self.tl // SCAN_BLOCK
        if reverse:
            mc = ncc - 1 - step
            ml = ncl - 1 - (step - ncc)
        else:
            mc = step
            ml = step - ncc
        return jnp.where(step < ncc, b * ncc + mc, self.bsz * ncc + b * ncl + ml)


def _mod_spec(lay, tm, layer, part, d, tile_off, grid_pos):
    def index(*ids):
        return (layer, lay.mod_row(ids[grid_pos] + tile_off, tm), part, 0, 0)

    return pl.BlockSpec((None, None, None, 1, d), index)


def _inproj_kernel(x_ref, g_ref, sc_ref, sh_ref, w_ref, o_ref):
    scale = g_ref[...] * (1.0 + sc_ref[...])
    for r0 in range(0, x_ref.shape[0], ROW_BLOCK):
        rows = slice(r0, r0 + ROW_BLOCK)
        h = _rms(x_ref[rows, :]) * scale + sh_ref[...]
        o_ref[rows, :] = _mm1(h, w_ref[...])


def _inproj(x_all, norm_g, mods, layer, w_bf16, tn, lay):
    n, d = x_all.shape
    cols = w_bf16.shape[1]
    tm = TM_MATMUL
    return pl.pallas_call(
        _inproj_kernel,
        grid=(cols // tn, n // tm),
        in_specs=[
            pl.BlockSpec((tm, d), lambda j, i: (i, 0)),
            pl.BlockSpec((1, d), lambda j, i: (0, 0)),
            _mod_spec(lay, tm, layer, 1, d, 0, 1),
            _mod_spec(lay, tm, layer, 0, d, 0, 1),
            pl.BlockSpec((d, tn), lambda j, i: (0, j)),
        ],
        out_specs=pl.BlockSpec((tm, tn), lambda j, i: (i, j)),
        out_shape=jax.ShapeDtypeStruct((n, cols), F32),
        compiler_params=_params("parallel", "parallel"),
        name="inproj",
    )(x_all, norm_g.reshape(1, d), mods, mods, w_bf16)


def _gmlp_kernel(p_ref, g_ref, ws_ref, wbt_ref, o_ref, *, heads):
    x = p_ref[...]
    ge = 0.5 * x * (1.0 + lax.erf(x * (1.0 / math.sqrt(2.0))))
    width = heads * A_HEAD_DIM
    tm = x.shape[0]
    for h in range(heads):
        lo, hi = h * A_HEAD_DIM, (h + 1) * A_HEAD_DIM
        vn = _rms(ge[:, width + lo:width + hi]) * g_ref[:, lo:hi]
        for c in range(tm // GMLP_CHUNK):
            r0, r1 = c * GMLP_CHUNK, (c + 1) * GMLP_CHUNK
            s = _mm1(ws_ref[h], vn[r0:r1]) + wbt_ref[:, h:h + 1]
            o_ref[r0:r1, lo:hi] = ge[r0:r1, lo:hi] * s


def _gmlp(pa, norm_g, ws, wb, tile_off, n_tiles):
    n, cols = pa.shape
    heads = ws.shape[0]
    width = cols // 2
    tm = TM_TOKEN
    return pl.pallas_call(
        functools.partial(_gmlp_kernel, heads=heads),
        grid=(n_tiles,),
        in_specs=[
            pl.BlockSpec((tm, cols), lambda i: (i + tile_off, 0)),
            pl.BlockSpec((1, width), lambda i: (0, 0)),
            pl.BlockSpec((heads, GMLP_CHUNK, GMLP_CHUNK), lambda i: (0, 0, 0)),
            pl.BlockSpec((GMLP_CHUNK, heads), lambda i: (0, 0)),
        ],
        out_specs=pl.BlockSpec((tm, width), lambda i: (i, 0)),
        out_shape=jax.ShapeDtypeStruct((n_tiles * tm, width), F32),
        compiler_params=_params("parallel"),
        name="gmlp",
    )(pa, norm_g.reshape(1, width), ws.astype(BF16), wb.T)


def _hgrn_kernel(qf_ref, if_ref, zf_ref, qb_ref, ib_ref, zb_ref, lb_ref, yf_ref, yb_ref, st_ref, *, heads):
    @pl.when(pl.program_id(1) == 0)
    def _():
        st_ref[...] = jnp.zeros_like(st_ref)

    cl, sub, hd = SCAN_CHUNK, HGRN_SUB, B_HEAD_DIM
    n_sub = qf_ref.shape[0] // cl
    row = lax.broadcasted_iota(jnp.int32, (cl, cl), 0)
    col = lax.broadcasted_iota(jnp.int32, (cl, cl), 1)

    def chunk(step, state):
        dirs = []
        for z, (q_ref, i_ref, z_ref) in enumerate(((qf_ref, if_ref, zf_ref), (qb_ref, ib_ref, zb_ref))):
            reverse = z == 1
            first = (n_sub - 1 - step if reverse else step) * cl
            rows = slice(first, first + cl)
            q = q_ref[rows, :]
            q = q * _sigmoid(q)
            v = i_ref[rows, :]
            zz = z_ref[rows, :]
            lb = lb_ref[z:z + 1]
            pos = zz > 0.0
            t = jnp.exp(-jnp.abs(zz))
            den = 1.0 + t
            k = (1.0 - lb) * jnp.where(pos, t, 1.0) / den
            num = jnp.where(pos, 1.0 + lb * t, lb + t)
            ok = num > 0.0
            lf = jnp.where(ok, jnp.log(jnp.where(ok, num, 1.0)), zz) - jnp.log(den)

            ut = (cl - 1 - row) if reverse else row
            us = (cl - 1 - col) if reverse else col
            sel = [us <= ut]
            keeps = []
            half = cl // 2
            while half >= sub:
                mid = (ut // (2 * half)) * (2 * half) + half
                sel.append(us < mid)
                keeps.append((ut >= mid) & (us < mid) & (us >= mid - half))
                half //= 2
            start = (ut // sub) * sub
            sel.append(us < start)
            keeps.append((us >= start) & (us <= ut))
            cums = _mm_exact_lhs(jnp.concatenate(sel, axis=0).astype(BF16), lf)
            c = cums[0:cl]
            total = c[0:1] if reverse else c[cl - 1:cl]
            dirs.append(dict(q=q, k=k, v=v, c=c, cums=cums, total=total, keeps=keeps, rows=rows,
                             qe=q * jnp.exp(c), ke=k * jnp.exp(total - c), scores=[None] * heads))

        for lvl in range(len(dirs[0]["keeps"])):
            for d in dirs:
                ref = d["cums"][(lvl + 1) * cl:(lvl + 2) * cl]
                dq, dk = d["c"] - ref, ref - d["c"]
                if lvl + 1 < len(d["keeps"]):
                    dq, dk = jnp.minimum(dq, 0.0), jnp.minimum(dk, 0.0)
                qs = d["q"] * jnp.exp(dq)
                ks = d["k"] * jnp.exp(dk)
                for h in range(heads):
                    lo, hi = h * hd, (h + 1) * hd
                    s = jnp.where(d["keeps"][lvl], _mm1(qs[:, lo:hi], ks[:, lo:hi], NT), 0.0)
                    d["scores"][h] = s if d["scores"][h] is None else d["scores"][h] + s

        new_state = []
        for z, (d, y_ref) in enumerate(zip(dirs, (yf_ref, yb_ref))):
            for h in range(heads):
                lo, hi = h * hd, (h + 1) * hd
                st = state[z * heads + h]
                y_ref[d["rows"], lo:hi] = (_mm1(d["qe"][:, lo:hi], st, NT)
                                           + _mm1(d["scores"][h], d["v"][:, lo:hi]))
                new_state.append(st * jnp.exp(d["total"][:, lo:hi])
                                 + _mm1(d["v"][:, lo:hi], d["ke"][:, lo:hi], TN))
        return new_state

    state = [st_ref[z, h] for z in range(2) for h in range(heads)]
    for step in range(n_sub):
        state = chunk(step, state)
    for i, s in enumerate(state):
        st_ref[i // heads, i % heads] = s


def _hgrn(pb, lb_dirs, lay):
    n = pb.shape[0]
    width = pb.shape[1] // 5
    heads = width // B_HEAD_DIM
    steps = (lay.tc + lay.tl) // SCAN_BLOCK
    blk = lambda comp, reverse: pl.BlockSpec(
        (SCAN_BLOCK, width), lambda b, s: (lay.scan_block(b, s, reverse), comp))
    out = jax.ShapeDtypeStruct((n, width), F32)
    return pl.pallas_call(
        functools.partial(_hgrn_kernel, heads=heads),
        grid=(lay.bsz, steps),
        in_specs=[blk(0, False), blk(1, False), blk(2, False), blk(0, True), blk(1, True), blk(3, True),
                  pl.BlockSpec((2, width), lambda b, s: (0, 0))],
        out_specs=[blk(0, False), blk(0, True)],
        out_shape=[out, out],
        scratch_shapes=[pltpu.VMEM((2, heads, B_HEAD_DIM, B_HEAD_DIM), F32)],
        compiler_params=_params("parallel", "arbitrary"),
        name="hgrn",
    )(pb, pb, pb, pb, pb, pb, lb_dirs)


def _rwkv_prep_kernel(cur_ref, prev_ref, next_ref, mu_ref, wlh_ref, wll_ref, guph_ref, gupl_ref, w0f_ref,
                      w0b_ref, a0_ref, kk_ref, ka_ref, rk_ref, seg_ref,
                      r_out, lwf_out, lwb_out, k_out, v_out, kk_out, ab_out, g_out, bonus_out,
                      *, ctx_tiles, rows_per_batch, width):
    tile = pl.program_id(0)
    z = cur_ref[...]
    tm, cols = z.shape
    halo = prev_ref.shape[0]
    ext = jnp.concatenate([prev_ref[...], z, next_ref[...]], axis=0)
    left = ext[halo - 1:halo - 1 + tm]
    right = ext[halo + 1:halo + 1 + tm]
    up = ext[0:tm]
    down = ext[2 * halo:2 * halo + tm]

    is_ctx = tile < ctx_tiles
    t = lax.broadcasted_iota(jnp.int32, (tm, 1), 0)
    rowb = t + ((tile - ctx_tiles) * tm) % rows_per_batch
    line = jnp.where(is_ctx, tm, GRID_W)
    pos = t & (line - 1)
    left = jnp.where(pos != 0, left, 0.0)
    right = jnp.where(pos != line - 1, right, 0.0)
    up = jnp.where(rowb >= GRID_W, up, 0.0)
    down = jnp.where(rowb < rows_per_batch - GRID_W, down, 0.0)
    slot = lax.broadcasted_iota(jnp.int32, (1, cols), 1) % 4
    slot = jnp.where(is_ctx, slot % 2, slot)
    shifted = jnp.where(slot == 0, left, jnp.where(slot == 1, right, jnp.where(slot == 2, up, down)))
    xx = z + mu_ref[...] * (shifted - z)

    r = xx[:, 0:width]
    k = xx[:, width:2 * width]
    v = xx[:, 2 * width:3 * width]
    lora = xx[:, 3 * width:3 * width + DECAY_LORA + AAA_LORA]
    lane = lax.broadcasted_iota(jnp.int32, (1, DECAY_LORA + AAA_LORA), 1)
    lora = jnp.where(lane < DECAY_LORA, jnp.tanh(lora), lora)
    up3 = _mm3_split_rhs(lora, wlh_ref[...], wll_ref[...])
    gd = xx[:, 3 * width + DECAY_LORA + AAA_LORA:]
    g_out[...] = _mm3_split_rhs(_sigmoid(gd), guph_ref[...], gupl_ref[...])
    lwf_out[...] = -DECAY_SCALE * _sigmoid(w0f_ref[...] + up3[:, 0:width])
    lwb_out[...] = -DECAY_SCALE * _sigmoid(w0b_ref[...] + up3[:, width:2 * width])
    a = _sigmoid(a0_ref[...] + up3[:, 2 * width:3 * width])

    seg = seg_ref[...]
    kk = k * kk_ref[...]
    kk = kk / jnp.maximum(jnp.sqrt(_head_sums(kk * kk, seg)), 1e-12)
    k2 = k * (1.0 + (a - 1.0) * ka_ref[...])
    r_out[...] = r
    k_out[...] = k2
    v_out[...] = v
    kk_out[...] = kk
    ab_out[...] = kk * a
    bonus_out[...] = _head_sums(r * k2 * rk_ref[...], seg) * v


def _rwkv_prep(pc, lay, mu, w_lora, g_up, w0, a0, k_k, k_a, r_k):
    n, cols = pc.shape
    width = a0.shape[-1]
    tm = TM_TOKEN
    assert lay.tc == tm and lay.tl % tm == 0 and tm % GRID_W == 0
    halo = GRID_W
    hb = tm // halo
    last = n // halo - 1
    row = lambda a: a.reshape(1, -1)
    seg = _head_block_diag(C_HEAD_DIM, 1.0)
    assert width % MXU_TILE == 0
    full = lambda a: pl.BlockSpec(a.shape, lambda i: (0,) * a.ndim)
    consts = [row(mu), *_split2(w_lora), *_split2(g_up), row(w0[0]), row(w0[1]), row(a0), row(k_k), row(k_a),
              row(r_k), seg]
    out_spec = pl.BlockSpec((tm, width), lambda i: (i, 0))
    return pl.pallas_call(
        functools.partial(_rwkv_prep_kernel, ctx_tiles=lay.n_ctx // tm, rows_per_batch=lay.tl, width=width),
        grid=(n // tm,),
        in_specs=[
            pl.BlockSpec((tm, cols), lambda i: (i, 0)),
            pl.BlockSpec((halo, cols), lambda i: (jnp.maximum(i * hb - 1, 0), 0)),
            pl.BlockSpec((halo, cols), lambda i: (jnp.minimum((i + 1) * hb, last), 0)),
        ] + [full(a) for a in consts],
        out_specs=[out_spec] * 9,
        out_shape=[jax.ShapeDtypeStruct((n, width), F32)] * 9,
        compiler_params=_params("parallel"),
        name="rwkv_prep",
    )(pc, pc, pc, *consts)


def _rwkv_kernel(rf_ref, kf_ref, vf_ref, kkf_ref, abf_ref, lwf_ref, rb_ref, kb_ref, vb_ref, kkb_ref, abb_ref,
                 lwb_ref, of_ref, ob_ref, st_ref, *, pairs):
    @pl.when(pl.program_id(1) == 0)
    def _():
        st_ref[...] = jnp.zeros_like(st_ref)

    cl, hd, pw = SCAN_CHUNK, C_HEAD_DIM, 2 * C_HEAD_DIM
    n_sub = rf_ref.shape[0] // cl
    cat = lambda a, b: jnp.concatenate([a, b], axis=0)
    cut = lambda x: [x[:, p * pw:(p + 1) * pw] for p in range(pairs)]
    row = lax.broadcasted_iota(jnp.int32, (cl, cl), 0)
    col = lax.broadcasted_iota(jnp.int32, (cl, cl), 1)
    lane = lax.broadcasted_iota(jnp.int32, (cl, pw), 1)
    m1 = (lane < hd).astype(F32)
    m2 = 1.0 - m1
    r2 = lax.broadcasted_iota(jnp.int32, (2 * cl, 2 * cl), 0)
    c2 = lax.broadcasted_iota(jnp.int32, (2 * cl, 2 * cl), 1)
    tt, ss = r2 % cl, c2 % cl
    same = (r2 // cl) == (c2 // cl)
    eye = jnp.where(r2 == c2, 1.0, 0.0)
    ti = lax.broadcasted_iota(jnp.int32, (cl, 2 * cl), 0)
    si = lax.broadcasted_iota(jnp.int32, (cl, 2 * cl), 1) % cl
    every = range(2 * pairs)

    def chunk(sub, s0):
        kkg, rg, kd, bd, kc, bc, vs, decay, strict, incl, rows = ([] for _ in range(11))
        for z, refs in enumerate(((rf_ref, kf_ref, vf_ref, kkf_ref, abf_ref, lwf_ref),
                                  (rb_ref, kb_ref, vb_ref, kkb_ref, abb_ref, lwb_ref))):
            reverse = z == 1
            first = (n_sub - 1 - sub if reverse else sub) * cl
            rows.append(slice(first, first + cl))
            r, k, v, kk, ab, lw = (ref[rows[z], :] for ref in refs)
            tri = (col >= row) if reverse else (col <= row)
            c = _mm_exact_lhs(tri.astype(BF16), lw)
            total = c[0:1] if reverse else c[cl - 1:cl]
            e_out = jnp.exp(-c)
            e_last = jnp.exp(total - c)
            kkg += cut(kk * jnp.exp(c - lw))
            rg += cut(r * jnp.exp(c))
            kd += cut(k * e_out)
            bd += cut(ab * e_out)
            kc += cut(k * e_last)
            bc += cut(ab * e_last)
            vs += cut(v)
            decay += cut(jnp.exp(total))
            strict += [(ss > tt) if reverse else (ss < tt)] * pairs
            incl += [(si >= ti) if reverse else (si <= ti)] * pairs

        g1 = [_mm1(cat(kkg[p] * m1, rg[p] * m1), cat(bd[p], kd[p]), NT) for p in every]
        g2 = [_mm1(cat(kkg[p] * m2, rg[p] * m2), cat(kd[p], bd[p]), NT) for p in every]
        xs = [_mm1(cat(kkg[p], rg[p]), s0[p], NT) for p in every]
        gl = [jnp.where(strict[p], cat(g1[p][0:cl], g2[p][0:cl]), 0.0) for p in every]
        l_diag = [jnp.where(same, gl[p], 0.0) for p in every]
        l_anti = [jnp.where(same, 0.0, gl[p]) for p in every]
        lkv = [_mm1(l_anti[p], cat(vs[p] * m2, vs[p] * m1)) for p in every]

        inv = [eye - l_diag[p] for p in every]
        pwr = [_mm1(l_diag[p], l_diag[p]) for p in every]
        n_sq = int(math.log2(cl)) - 1
        for i in range(n_sq):
            inv = [inv[p] + _mm1(inv[p], pwr[p]) for p in every]
            if i + 1 < n_sq:
                pwr = [_mm1(pwr[p], pwr[p]) for p in every]

        rhs = [xs[p][0:cl] + lkv[p][0:cl] + lkv[p][cl:] for p in every]
        ust = [_mm1(inv[p], cat(rhs[p] * m1, rhs[p] * m2)) for p in every]
        u = [ust[p][0:cl] + ust[p][cl:] for p in every]
        o1 = [_mm1(jnp.where(incl[p], g1[p][cl:], 0.0), cat(-u[p] * m1, vs[p] * m1)) for p in every]
        o2 = [_mm1(jnp.where(incl[p], g2[p][cl:], 0.0), cat(vs[p] * m2, -u[p] * m2)) for p in every]
        upd = [_mm1(cat(vs[p], -u[p]), cat(kc[p], bc[p]), TN) for p in every]
        for p in every:
            z, q = divmod(p, pairs)
            o_ref = ob_ref if z else of_ref
            o_ref[rows[z], q * pw:(q + 1) * pw] = xs[p][cl:] + o1[p] + o2[p]
        return [s0[p] * decay[p] + jnp.where(same, upd[p], 0.0) for p in every]

    state = [st_ref[p // pairs, p % pairs] for p in every]
    for sub in range(n_sub):
        state = chunk(sub, state)
    for p in every:
        st_ref[p // pairs, p % pairs] = state[p]


def _rwkv_scan(r, lwf, lwb, k, v, kk, ab, lay):
    n, width = r.shape
    pairs = width // (2 * C_HEAD_DIM)
    steps = (lay.tc + lay.tl) // SCAN_BLOCK
    blk = lambda reverse: pl.BlockSpec(
        (SCAN_BLOCK, width), lambda b, s: (lay.scan_block(b, s, reverse), 0))
    out = jax.ShapeDtypeStruct((n, width), F32)
    return pl.pallas_call(
        functools.partial(_rwkv_kernel, pairs=pairs),
        grid=(lay.bsz, steps),
        in_specs=[blk(False)] * 6 + [blk(True)] * 6,
        out_specs=[blk(False), blk(True)],
        out_shape=[out, out],
        scratch_shapes=[pltpu.VMEM((2, pairs, 2 * C_HEAD_DIM, 2 * C_HEAD_DIM), F32)],
        compiler_params=_params("parallel", "arbitrary"),
        name="rwkv",
    )(r, k, v, kk, ab, lwf, r, k, v, kk, ab, lwb)


def _outproj_kernel(x_ref, ya_ref, hf_ref, hb_ref, og_ref, hg_ref, rf_ref, rb_ref, bonus_ref, gate_ref,
                    gng_ref, gnb_ref, avg_ref, w_ref, g1_ref, n2_ref, sc2_ref, sh2_ref, rw_ref, rbias_ref,
                    xo_ref, h2_ref, lg_ref, *, a_width, b_width):
    avg = avg_ref[...]
    scale2 = n2_ref[...] * (1.0 + sc2_ref[...])
    for r0 in range(0, x_ref.shape[0], ROW_BLOCK):
        rows = slice(r0, r0 + ROW_BLOCK)
        o = hf_ref[rows, :] + hb_ref[rows, :]
        og = og_ref[rows, :]
        yb = []
        for h in range(b_width // B_HEAD_DIM):
            lo, hi = h * B_HEAD_DIM, (h + 1) * B_HEAD_DIM
            yb.append(_rms(o[:, lo:hi]) * hg_ref[:, lo:hi] * _sigmoid(og[:, lo:hi]))
        yb = jnp.concatenate(yb, axis=1)

        rw = rf_ref[rows, :] + rb_ref[rows, :]
        xc = rw - _head_sums(rw, avg)
        var = _head_sums(xc * xc, avg)
        yc = ((xc * lax.rsqrt(var + RWKV_GN_EPS) * gng_ref[...] + gnb_ref[...] + bonus_ref[rows, :])
              * gate_ref[rows, :])

        y = (_mm1(ya_ref[rows, :], w_ref[0:a_width]) + _mm1(yb, w_ref[a_width:a_width + b_width])
             + _mm1(yc, w_ref[a_width + b_width:]))
        xn = x_ref[rows, :] + g1_ref[...] * y
        xo_ref[rows, :] = xn
        h2 = _rms(xn) * scale2 + sh2_ref[...]
        h2_ref[rows, :] = h2.astype(BF16)
        lg_ref[:, rows] = _mm3(rw_ref[...], h2, NT) + rbias_ref[...]


def _outproj(x_all, ya, hf, hb, pb, hgrn_g, rf, rb, bonus, gate, gn_g, gn_b, w_out_bf16, mods, layer,
             norm2_g, router_wt, router_b_col, lay, tile_off, n_tiles):
    n, d = x_all.shape
    a_width = ya.shape[1]
    b_width = hf.shape[1]
    c_width = rf.shape[1]
    tm = TM_TOKEN
    avg = _head_block_diag(C_HEAD_DIM, 1.0 / C_HEAD_DIM)
    assert c_width % MXU_TILE == 0
    tok = lambda w, comp=0, off=tile_off: pl.BlockSpec((tm, w), lambda i: (i + off, comp))
    full = lambda a: pl.BlockSpec(a.shape, lambda i: (0,) * a.ndim)
    row = lambda a: a.reshape(1, -1)
    mod = lambda part: _mod_spec(lay, tm, layer, part, d, tile_off, 0)
    hg, gg, gb, n2 = row(hgrn_g), row(gn_g), row(gn_b), row(norm2_g)
    rows = n_tiles * tm
    return pl.pallas_call(
        functools.partial(_outproj_kernel, a_width=a_width, b_width=b_width),
        grid=(n_tiles,),
        in_specs=[tok(d), tok(a_width, off=0), tok(b_width), tok(b_width), tok(b_width, 4), full(hg),
                  tok(c_width), tok(c_width), tok(c_width), tok(c_width), full(gg), full(gb), full(avg),
                  full(w_out_bf16), mod(2), full(n2), mod(4), mod(3), full(router_wt), full(router_b_col)],
        out_specs=[tok(d, off=0), tok(d, off=0), pl.BlockSpec((N_EXPERTS, tm), lambda i: (0, i))],
        out_shape=[jax.ShapeDtypeStruct((rows, d), F32), jax.ShapeDtypeStruct((rows, d), BF16),
                   jax.ShapeDtypeStruct((N_EXPERTS, rows), F32)],
        compiler_params=_params("parallel"),
        name="outproj",
    )(x_all, ya, hf, hb, pb, hg, rf, rb, bonus, gate, gg, gb, avg, w_out_bf16, mods, n2, mods, mods,
      router_wt, router_b_col)


def _route(logits):
    per_group = N_EXPERTS // N_EXPERT_GROUPS
    assert per_group == 4
    expert = lax.broadcasted_iota(jnp.int32, logits.shape, 0)
    ex = jnp.exp(logits - jnp.max(logits, axis=0, keepdims=True))
    p = ex / jnp.sum(ex, axis=0, keepdims=True)

    def top2_sum(a, b, c, d):
        h1, l1, h2, l2 = jnp.maximum(a, b), jnp.minimum(a, b), jnp.maximum(c, d), jnp.minimum(c, d)
        return jnp.maximum(h1, h2) + jnp.maximum(jnp.minimum(h1, h2), jnp.maximum(l1, l2))

    score = [top2_sum(*(p[g * per_group + i:g * per_group + i + 1] for i in range(per_group)))
             for g in range(N_EXPERT_GROUPS)]
    best, sel = score[0], jnp.zeros(score[0].shape, jnp.int32)
    for g in range(1, N_EXPERT_GROUPS):
        better = score[g] > best
        best = jnp.where(better, score[g], best)
        sel = jnp.where(better, g, sel)
    pm = jnp.where(expert // per_group == sel, p, -1.0)

    def first_max(vals):
        top = jnp.max(vals, axis=0, keepdims=True)
        return top, jnp.min(jnp.where(vals == top, expert, N_EXPERTS), axis=0, keepdims=True)

    p1, i1 = first_max(pm)
    p2, i2 = first_max(jnp.where(expert == i1, -2.0, pm))
    tot = p1 + p2
    gates = jnp.where(expert == i1, p1 / tot, 0.0) + jnp.where(expert == i2, p2 / tot, 0.0)
    return gates, (expert == i1) | (expert == i2)


def _moe_kernel(h_ref, lg_ref, x_ref, g2_ref, fg_ref, wgu_ref, wd_ref, o_ref, gate_ref, rank_ref,
                *, d_expert, rows, final_norm):
    e = pl.program_id(1)
    tm = h_ref.shape[0]

    @pl.when(e == 0)
    def _():
        o_ref[...] = x_ref[...]
        gates, chosen = _route(lg_ref[...])
        lane = lax.broadcasted_iota(jnp.int32, chosen.shape, 1)
        count = jnp.where(chosen, 1.0, 0.0)
        step = 1
        while step < tm:
            count = count + jnp.where(lane >= step, pltpu.roll(count, step, axis=1), 0.0)
            step *= 2
        gate_ref[...] = gates
        rank_ref[...] = jnp.where(chosen, count - 1.0, -1.0)

    g_row = gate_ref[pl.ds(e, 1), :]
    rank_row = rank_ref[pl.ds(e, 1), :]
    n_sel = jnp.max(rank_row, axis=1, keepdims=True)[0, 0].astype(jnp.int32) + 1

    def one_pass(j):
        slot = (lax.broadcasted_iota(jnp.int32, (rows, tm), 0) + j * rows).astype(F32)
        hit = rank_row == slot
        pick = jnp.where(hit, 1.0, 0.0).astype(BF16)
        hg = _dot(pick, h_ref[...]).astype(BF16)
        gu = _dot(hg, wgu_ref[...])
        gt, up = gu[:, :d_expert], gu[:, d_expert:]
        y = _mm1(gt * _sigmoid(gt) * up, wd_ref[...])
        gcol = jnp.sum(jnp.where(hit, g_row, 0.0), axis=1, keepdims=True)
        o_ref[...] += _dot(pick, (y * gcol * g2_ref[...]).astype(BF16), TN)
        return j + 1

    lax.while_loop(lambda j: j * rows < n_sel, one_pass, jnp.int32(0))

    if final_norm:
        @pl.when(e == pl.num_programs(1) - 1)
        def _():
            o_ref[...] = _rms(o_ref[...]) * fg_ref[...]


def _moe(h2, logits_t, x_mid, mods, layer, wgu_bf16, wd_bf16, final_g, lay, row_off, final_norm):
    n, d = h2.shape
    n_exp, _, two_de = wgu_bf16.shape
    tm = MOE_TM
    assert n % tm == 0 and row_off % tm == 0 and lay.n_ctx % tm == 0 and lay.tl % tm == 0
    once = pl.Buffered(1)
    tok = pl.BlockSpec((tm, d), lambda i, e: (i, 0), pipeline_mode=once)
    return pl.pallas_call(
        functools.partial(_moe_kernel, d_expert=two_de // 2, rows=MOE_ROWS, final_norm=final_norm),
        grid=(n // tm, n_exp),
        in_specs=[tok, pl.BlockSpec((n_exp, tm), lambda i, e: (0, i)), tok,
                  _mod_spec(lay, tm, layer, 5, d, row_off // tm, 0),
                  pl.BlockSpec((1, d), lambda i, e: (0, 0)),
                  pl.BlockSpec((None, d, two_de), lambda i, e: (e, 0, 0)),
                  pl.BlockSpec((None, two_de // 2, d), lambda i, e: (e, 0, 0))],
        out_specs=tok,
        out_shape=jax.ShapeDtypeStruct((n, d), F32),
        scratch_shapes=[pltpu.VMEM((n_exp, tm), F32), pltpu.VMEM((n_exp, tm), F32)],
        compiler_params=_params("parallel", "arbitrary"),
        name="moe",
    )(h2, logits_t, x_mid, mods, final_g.reshape(1, d), wgu_bf16, wd_bf16)


def kernel(x, c, ctx, c_ctx, ada_w, ada_b, norm1_g, norm2_g, w_in, w_out, gmlp_norm_g, gmlp_ws, gmlp_b,
           hgrn_lb_logits, hgrn_norm_g, rwkv_mu, rwkv_w0, rwkv_w_up, rwkv_a0, rwkv_a_up, rwkv_g_up,
           rwkv_k_k, rwkv_k_a, rwkv_r_k, rwkv_gn_g, rwkv_gn_b, router_w, router_b, moe_w_gate_up,
           moe_w_down, final_norm_g):
    bsz, tl, d = x.shape
    tc = ctx.shape[1]
    depth = ada_w.shape[0]
    lay = _Layout(bsz, tc, tl)
    a_cols = 2 * gmlp_norm_g.shape[1]
    b_cols = 5 * hgrn_norm_g.shape[1]
    c_width = rwkv_a0.shape[1]
    assert lay.n_ctx % TM_MATMUL == 0 and tl % TM_MATMUL == 0 and tc % SCAN_BLOCK == 0

    mod_rows = -(-(bsz + 1) // 8) * 8
    c_rows = jnp.zeros((mod_rows, d), F32).at[:bsz].set(c).at[bsz].set(c_ctx)
    mods = _adaln(c_rows, ada_w, ada_b).reshape(depth, mod_rows, 6, 1, d)

    lbs = jnp.cumsum(jax.nn.softmax(hgrn_lb_logits.astype(F32), axis=1), axis=1)
    lbs = lbs - lbs[:, :1]
    w_in_bf16 = w_in.astype(BF16)
    w_out_bf16 = w_out.astype(BF16)
    wgu_bf16 = moe_w_gate_up.astype(BF16)
    wd_bf16 = moe_w_down.astype(BF16)
    router_wt = router_w.T
    router_b_col = router_b.reshape(N_EXPERTS, 1)

    x_all = jnp.concatenate([ctx.reshape(lay.n_ctx, d), x.reshape(lay.n_lat, d)], axis=0)
    ctx_tiles = lay.n_ctx // TM_TOKEN
    all_tiles = lay.n // TM_TOKEN

    for l in range(depth):
        last = l == depth - 1
        tile_off = ctx_tiles if last else 0
        n_tiles = all_tiles - tile_off
        w_l = w_in_bf16[l]
        pa = _inproj(x_all, norm1_g[l], mods, l, w_l[:, :a_cols], a_cols, lay)
        pb = _inproj(x_all, norm1_g[l], mods, l, w_l[:, a_cols:a_cols + b_cols], b_cols // 3, lay)
        pc = _inproj(x_all, norm1_g[l], mods, l, w_l[:, a_cols + b_cols:], (w_l.shape[1] - a_cols - b_cols) // 2, lay)

        ya = _gmlp(pa, gmlp_norm_g[l], gmlp_ws[l], gmlp_b[l], tile_off, n_tiles)
        hf, hb = _hgrn(pb, lbs[:, l], lay)

        w_lora = jnp.zeros((DECAY_LORA + AAA_LORA, 3 * c_width), F32)
        w_lora = w_lora.at[:DECAY_LORA, :c_width].set(rwkv_w_up[l, 0])
        w_lora = w_lora.at[:DECAY_LORA, c_width:2 * c_width].set(rwkv_w_up[l, 1])
        w_lora = w_lora.at[DECAY_LORA:, 2 * c_width:].set(rwkv_a_up[l])
        r, lwf, lwb, k2, v, kk, ab, gate, bonus = _rwkv_prep(
            pc, lay, rwkv_mu[l], w_lora, rwkv_g_up[l], rwkv_w0[l], rwkv_a0[l], rwkv_k_k[l], rwkv_k_a[l],
            rwkv_r_k[l])
        rf, rb = _rwkv_scan(r, lwf, lwb, k2, v, kk, ab, lay)

        x_mid, h2, logits_t = _outproj(
            x_all, ya, hf, hb, pb, hgrn_norm_g[l], rf, rb, bonus, gate, rwkv_gn_g[l], rwkv_gn_b[l],
            w_out_bf16[l], mods, l, norm2_g[l], router_wt, router_b_col, lay, tile_off, n_tiles)
        x_all = _moe(h2, logits_t, x_mid, mods, l, wgu_bf16[l], wd_bf16[l], final_norm_g, lay,
                     tile_off * TM_TOKEN, final_norm=last)
    return x_all.reshape(bsz, tl, d)
```

```python
import functools
import math

import jax
import jax.numpy as jnp
from jax import lax
from jax.experimental import pallas as pl
from jax.experimental.pallas import tpu as pltpu

F32 = jnp.float32
BF16 = jnp.bfloat16

GRID_W = 64
A_HEAD_DIM = 128
B_HEAD_DIM = 128
C_HEAD_DIM = 64
GMLP_CHUNK = 128
DECAY_LORA = 64
AAA_LORA = 64
GATE_LORA = 128
N_EXPERTS = 16
N_EXPERT_GROUPS = 4
NORM_EPS = 1e-6
RWKV_GN_EPS = 64e-5
DECAY_SCALE = math.exp(-0.5)

LANES = 128
MXU_TILE = 256
SCAN_CHUNK = 64
SCAN_BLOCK = 256
HGRN_SUB = 8
TM_MATMUL = 512
TM_TOKEN = 256
ROW_BLOCK = 128
MOE_TM = 1024
MOE_ROWS = 160
VMEM_LIMIT = 56 * 1024 * 1024

NN = (((1,), (0,)), ((), ()))
NT = (((1,), (1,)), ((), ()))
TN = (((0,), (0,)), ((), ()))


def _dot(a, b, dims=NN):
    return lax.dot_general(a, b, dims, preferred_element_type=F32)


def _mm1(a, b, dims=NN):
    return _dot(a.astype(BF16), b.astype(BF16), dims)


def _split2(x):
    hi = x.astype(BF16)
    lo = (x - hi.astype(F32)).astype(BF16)
    return hi, lo


def _split3(x):
    hi = x.astype(BF16)
    r1 = x - hi.astype(F32)
    mid = r1.astype(BF16)
    lo = (r1 - mid.astype(F32)).astype(BF16)
    return hi, mid, lo


def _mm3(a, b, dims=NN):
    ah, al = _split2(a)
    bh, bl = _split2(b)
    return _dot(ah, bh, dims) + (_dot(al, bh, dims) + _dot(ah, bl, dims))


def _mm3_split_rhs(a, bh, bl):
    ah, al = _split2(a)
    return _dot(ah, bh) + (_dot(al, bh) + _dot(ah, bl))


def _mm_exact_lhs(a_bf16, b, dims=NN):
    h, m, l = _split3(b)
    return _dot(a_bf16, h, dims) + (_dot(a_bf16, m, dims) + _dot(a_bf16, l, dims))


def _head_sums(x, seg):
    hi, lo = _split2(x)
    w = seg.shape[0]
    return jnp.concatenate(
        [_dot(hi[:, g:g + w], seg) + _dot(lo[:, g:g + w], seg) for g in range(0, x.shape[1], w)], axis=1)


def _head_block_diag(head_dim, value):
    hi = lax.broadcasted_iota(jnp.int32, (MXU_TILE, MXU_TILE), 0) // head_dim
    hj = lax.broadcasted_iota(jnp.int32, (MXU_TILE, MXU_TILE), 1) // head_dim
    return jnp.where(hi == hj, value, 0.0).astype(BF16)


def _sigmoid(x):
    return jax.nn.sigmoid(x)


def _rms(x):
    return x * lax.rsqrt(jnp.mean(x * x, axis=-1, keepdims=True) + NORM_EPS)


def _params(*sem):
    return pltpu.CompilerParams(dimension_semantics=sem, vmem_limit_bytes=VMEM_LIMIT)


def _adaln_kernel(c_ref, w_ref, b_ref, o_ref):
    c = c_ref[...]
    o_ref[...] = _mm3(c * _sigmoid(c), w_ref[...]) + b_ref[...]


def _adaln(c_rows, ada_w, ada_b):
    depth, d, n = ada_w.shape
    rows = c_rows.shape[0]
    tn = 1024
    return pl.pallas_call(
        _adaln_kernel,
        grid=(depth, n // tn),
        in_specs=[
            pl.BlockSpec((rows, d), lambda l, j: (0, 0)),
            pl.BlockSpec((None, d, tn), lambda l, j: (l, 0, j)),
            pl.BlockSpec((None, 1, tn), lambda l, j: (l, 0, j)),
        ],
        out_specs=pl.BlockSpec((None, rows, tn), lambda l, j: (l, 0, j)),
        out_shape=jax.ShapeDtypeStruct((depth, rows, n), F32),
        compiler_params=_params("parallel", "parallel"),
        name="adaln",
    )(c_rows, ada_w, ada_b.reshape(depth, 1, n))


class _Layout:
    def __init__(self, bsz, tc, tl):
        self.bsz, self.tc, self.tl = bsz, tc, tl
        self.n_ctx = bsz * tc
        self.n_lat = bsz * tl
        self.n = self.n_ctx + self.n_lat

    def mod_row(self, tile, tm):
        ctx_tiles = self.n_ctx // tm
        return jnp.where(tile < ctx_tiles, self.bsz, (tile - ctx_tiles) // (self.tl // tm))

    def scan_block(self, b, step, reverse):
        ncc = self.tc // SCAN_BLOCK
        ncl = self.tl // SCAN_BLOCK
        if reverse:
            mc = ncc - 1 - step
            ml = ncl - 1 - (step - ncc)
        else:
            mc = step
            ml = step - ncc
        return jnp.where(step < ncc, b * ncc + mc, self.bsz * ncc + b * ncl + ml)


def _mod_spec(lay, tm, layer, part, d, tile_off, grid_pos):
    def index(*ids):
        return (layer, lay.mod_row(ids[grid_pos] + tile_off, tm), part, 0, 0)

    return pl.BlockSpec((None, None, None, 1, d), index)


def _inproj_kernel(x_ref, g_ref, sc_ref, sh_ref, w_ref, o_ref):
    scale = g_ref[...] * (1.0 + sc_ref[...])
    for r0 in range(0, x_ref.shape[0], ROW_BLOCK):
        rows = slice(r0, r0 + ROW_BLOCK)
        h = _rms(x_ref[rows, :]) * scale + sh_ref[...]
        o_ref[rows, :] = _mm1(h, w_ref[...])


def _inproj(x_all, norm_g, mods, layer, w_bf16, tn, lay):
    n, d = x_all.shape
    cols = w_bf16.shape[1]
    tm = TM_MATMUL
    return pl.pallas_call(
        _inproj_kernel,
        grid=(cols // tn, n // tm),
        in_specs=[
            pl.BlockSpec((tm, d), lambda j, i: (i, 0)),
            pl.BlockSpec((1, d), lambda j, i: (0, 0)),
            _mod_spec(lay, tm, layer, 1, d, 0, 1),
            _mod_spec(lay, tm, layer, 0, d, 0, 1),
            pl.BlockSpec((d, tn), lambda j, i: (0, j)),
        ],
        out_specs=pl.BlockSpec((tm, tn), lambda j, i: (i, j)),
        out_shape=jax.ShapeDtypeStruct((n, cols), F32),
        compiler_params=_params("parallel", "parallel"),
        name="inproj",
    )(x_all, norm_g.reshape(1, d), mods, mods, w_bf16)


def _gmlp_kernel(p_ref, g_ref, ws_ref, wbt_ref, o_ref, *, heads):
    x = p_ref[...]
    ge = 0.5 * x * (1.0 + lax.erf(x * (1.0 / math.sqrt(2.0))))
    width = heads * A_HEAD_DIM
    tm = x.shape[0]
    for h in range(heads):
        lo, hi = h * A_HEAD_DIM, (h + 1) * A_HEAD_DIM
        vn = _rms(ge[:, width + lo:width + hi]) * g_ref[:, lo:hi]
        for c in range(tm // GMLP_CHUNK):
            r0, r1 = c * GMLP_CHUNK, (c + 1) * GMLP_CHUNK
            s = _mm1(ws_ref[h], vn[r0:r1]) + wbt_ref[:, h:h + 1]
            o_ref[r0:r1, lo:hi] = ge[r0:r1, lo:hi] * s


def _gmlp(pa, norm_g, ws, wb, tile_off, n_tiles):
    n, cols = pa.shape
    heads = ws.shape[0]
    width = cols // 2
    tm = TM_TOKEN
    return pl.pallas_call(
        functools.partial(_gmlp_kernel, heads=heads),
        grid=(n_tiles,),
        in_specs=[
            pl.BlockSpec((tm, cols), lambda i: (i + tile_off, 0)),
            pl.BlockSpec((1, width), lambda i: (0, 0)),
            pl.BlockSpec((heads, GMLP_CHUNK, GMLP_CHUNK), lambda i: (0, 0, 0)),
            pl.BlockSpec((GMLP_CHUNK, heads), lambda i: (0, 0)),
        ],
        out_specs=pl.BlockSpec((tm, width), lambda i: (i, 0)),
        out_shape=jax.ShapeDtypeStruct((n_tiles * tm, width), F32),
        compiler_params=_params("parallel"),
        name="gmlp",
    )(pa, norm_g.reshape(1, width), ws.astype(BF16), wb.T)


def _hgrn_kernel(qf_ref, if_ref, zf_ref, qb_ref, ib_ref, zb_ref, lb_ref, yf_ref, yb_ref, st_ref, *, heads):
    @pl.when(pl.program_id(1) == 0)
    def _():
        st_ref[...] = jnp.zeros_like(st_ref)

    cl, sub, hd = SCAN_CHUNK, HGRN_SUB, B_HEAD_DIM
    n_sub = qf_ref.shape[0] // cl
    row = lax.broadcasted_iota(jnp.int32, (cl, cl), 0)
    col = lax.broadcasted_iota(jnp.int32, (cl, cl), 1)

    def chunk(step, state):
        dirs = []
        for z, (q_ref, i_ref, z_ref) in enumerate(((qf_ref, if_ref, zf_ref), (qb_ref, ib_ref, zb_ref))):
            reverse = z == 1
            first = (n_sub - 1 - step if reverse else step) * cl
            rows = slice(first, first + cl)
            q = q_ref[rows, :]
            q = q * _sigmoid(q)
            v = i_ref[rows, :]
            zz = z_ref[rows, :]
            lb = lb_ref[z:z + 1]
            pos = zz > 0.0
            t = jnp.exp(-jnp.abs(zz))
            den = 1.0 + t
            k = (1.0 - lb) * jnp.where(pos, t, 1.0) / den
            num = jnp.where(pos, 1.0 + lb * t, lb + t)
            ok = num > 0.0
            lf = jnp.where(ok, jnp.log(jnp.where(ok, num, 1.0)), zz) - jnp.log(den)

            ut = (cl - 1 - row) if reverse else row
            us = (cl - 1 - col) if reverse else col
            sel = [us <= ut]
            keeps = []
            half = cl // 2
            while half >= sub:
                mid = (ut // (2 * half)) * (2 * half) + half
                sel.append(us < mid)
                keeps.append((ut >= mid) & (us < mid) & (us >= mid - half))
                half //= 2
            start = (ut // sub) * sub
            sel.append(us < start)
            keeps.append((us >= start) & (us <= ut))
            cums = _mm_exact_lhs(jnp.concatenate(sel, axis=0).astype(BF16), lf)
            c = cums[0:cl]
            total = c[0:1] if reverse else c[cl - 1:cl]
            dirs.append(dict(q=q, k=k, v=v, c=c, cums=cums, total=total, keeps=keeps, rows=rows,
                             qe=q * jnp.exp(c), ke=k * jnp.exp(total - c), scores=[None] * heads))

        for lvl in range(len(dirs[0]["keeps"])):
            for d in dirs:
                ref = d["cums"][(lvl + 1) * cl:(lvl + 2) * cl]
                dq, dk = d["c"] - ref, ref - d["c"]
                if lvl + 1 < len(d["keeps"]):
                    dq, dk = jnp.minimum(dq, 0.0), jnp.minimum(dk, 0.0)
                qs = d["q"] * jnp.exp(dq)
                ks = d["k"] * jnp.exp(dk)
                for h in range(heads):
                    lo, hi = h * hd, (h + 1) * hd
                    s = jnp.where(d["keeps"][lvl], _mm1(qs[:, lo:hi], ks[:, lo:hi], NT), 0.0)
                    d["scores"][h] = s if d["scores"][h] is None else d["scores"][h] + s

        new_state = []
        for z, (d, y_ref) in enumerate(zip(dirs, (yf_ref, yb_ref))):
            for h in range(heads):
                lo, hi = h * hd, (h + 1) * hd
                st = state[z * heads + h]
                y_ref[d["rows"], lo:hi] = (_mm1(d["qe"][:, lo:hi], st, NT)
                                           + _mm1(d["scores"][h], d["v"][:, lo:hi]))
                new_state.append(st * jnp.exp(d["total"][:, lo:hi])
                                 + _mm1(d["v"][:, lo:hi], d["ke"][:, lo:hi], TN))
        return new_state

    state = [st_ref[z, h] for z in range(2) for h in range(heads)]
    for step in range(n_sub):
        state = chunk(step, state)
    for i, s in enumerate(state):
        st_ref[i // heads, i % heads] = s


def _hgrn(pb, lb_dirs, lay):
    n = pb.shape[0]
    width = pb.shape[1] // 5
    heads = width // B_HEAD_DIM
    steps = (lay.tc + lay.tl) // SCAN_BLOCK
    blk = lambda comp, reverse: pl.BlockSpec(
        (SCAN_BLOCK, width), lambda b, s: (lay.scan_block(b, s, reverse), comp))
    out = jax.ShapeDtypeStruct((n, width), F32)
    return pl.pallas_call(
        functools.partial(_hgrn_kernel, heads=heads),
        grid=(lay.bsz, steps),
        in_specs=[blk(0, False), blk(1, False), blk(2, False), blk(0, True), blk(1, True), blk(3, True),
                  pl.BlockSpec((2, width), lambda b, s: (0, 0))],
        out_specs=[blk(0, False), blk(0, True)],
        out_shape=[out, out],
        scratch_shapes=[pltpu.VMEM((2, heads, B_HEAD_DIM, B_HEAD_DIM), F32)],
        compiler_params=_params("parallel", "arbitrary"),
        name="hgrn",
    )(pb, pb, pb, pb, pb, pb, lb_dirs)


def _rwkv_prep_kernel(cur_ref, prev_ref, next_ref, mu_ref, wlh_ref, wll_ref, guph_ref, gupl_ref, w0f_ref,
                      w0b_ref, a0_ref, kk_ref, ka_ref, rk_ref, seg_ref,
                      r_out, lwf_out, lwb_out, k_out, v_out, kk_out, ab_out, g_out, bonus_out,
                      *, ctx_tiles, rows_per_batch, width):
    tile = pl.program_id(0)
    z = cur_ref[...]
    tm, cols = z.shape
    halo = prev_ref.shape[0]
    ext = jnp.concatenate([prev_ref[...], z, next_ref[...]], axis=0)
    left = ext[halo - 1:halo - 1 + tm]
    right = ext[halo + 1:halo + 1 + tm]
    up = ext[0:tm]
    down = ext[2 * halo:2 * halo + tm]

    is_ctx = tile < ctx_tiles
    t = lax.broadcasted_iota(jnp.int32, (tm, 1), 0)
    rowb = t + ((tile - ctx_tiles) * tm) % rows_per_batch
    line = jnp.where(is_ctx, tm, GRID_W)
    pos = t & (line - 1)
    left = jnp.where(pos != 0, left, 0.0)
    right = jnp.where(pos != line - 1, right, 0.0)
    up = jnp.where(rowb >= GRID_W, up, 0.0)
    down = jnp.where(rowb < rows_per_batch - GRID_W, down, 0.0)
    slot = lax.broadcasted_iota(jnp.int32, (1, cols), 1) % 4
    slot = jnp.where(is_ctx, slot % 2, slot)
    shifted = jnp.where(slot == 0, left, jnp.where(slot == 1, right, jnp.where(slot == 2, up, down)))
    xx = z + mu_ref[...] * (shifted - z)

    r = xx[:, 0:width]
    k = xx[:, width:2 * width]
    v = xx[:, 2 * width:3 * width]
    lora = xx[:, 3 * width:3 * width + DECAY_LORA + AAA_LORA]
    lane = lax.broadcasted_iota(jnp.int32, (1, DECAY_LORA + AAA_LORA), 1)
    lora = jnp.where(lane < DECAY_LORA, jnp.tanh(lora), lora)
    up3 = _mm3_split_rhs(lora, wlh_ref[...], wll_ref[...])
    gd = xx[:, 3 * width + DECAY_LORA + AAA_LORA:]
    g_out[...] = _mm3_split_rhs(_sigmoid(gd), guph_ref[...], gupl_ref[...])
    lwf_out[...] = -DECAY_SCALE * _sigmoid(w0f_ref[...] + up3[:, 0:width])
    lwb_out[...] = -DECAY_SCALE * _sigmoid(w0b_ref[...] + up3[:, width:2 * width])
    a = _sigmoid(a0_ref[...] + up3[:, 2 * width:3 * width])

    seg = seg_ref[...]
    kk = k * kk_ref[...]
    kk = kk / jnp.maximum(jnp.sqrt(_head_sums(kk * kk, seg)), 1e-12)
    k2 = k * (1.0 + (a - 1.0) * ka_ref[...])
    r_out[...] = r
    k_out[...] = k2
    v_out[...] = v
    kk_out[...] = kk
    ab_out[...] = kk * a
    bonus_out[...] = _head_sums(r * k2 * rk_ref[...], seg) * v


def _rwkv_prep(pc, lay, mu, w_lora, g_up, w0, a0, k_k, k_a, r_k):
    n, cols = pc.shape
    width = a0.shape[-1]
    tm = TM_TOKEN
    assert lay.tc == tm and lay.tl % tm == 0 and tm % GRID_W == 0
    halo = GRID_W
    hb = tm // halo
    last = n // halo - 1
    row = lambda a: a.reshape(1, -1)
    seg = _head_block_diag(C_HEAD_DIM, 1.0)
    assert width % MXU_TILE == 0
    full = lambda a: pl.BlockSpec(a.shape, lambda i: (0,) * a.ndim)
    consts = [row(mu), *_split2(w_lora), *_split2(g_up), row(w0[0]), row(w0[1]), row(a0), row(k_k), row(k_a),
              row(r_k), seg]
    out_spec = pl.BlockSpec((tm, width), lambda i: (i, 0))
    return pl.pallas_call(
        functools.partial(_rwkv_prep_kernel, ctx_tiles=lay.n_ctx // tm, rows_per_batch=lay.tl, width=width),
        grid=(n // tm,),
        in_specs=[
            pl.BlockSpec((tm, cols), lambda i: (i, 0)),
            pl.BlockSpec((halo, cols), lambda i: (jnp.maximum(i * hb - 1, 0), 0)),
            pl.BlockSpec((halo, cols), lambda i: (jnp.minimum((i + 1) * hb, last), 0)),
        ] + [full(a) for a in consts],
        out_specs=[out_spec] * 9,
        out_shape=[jax.ShapeDtypeStruct((n, width), F32)] * 9,
        compiler_params=_params("parallel"),
        name="rwkv_prep",
    )(pc, pc, pc, *consts)


def _rwkv_kernel(rf_ref, kf_ref, vf_ref, kkf_ref, abf_ref, lwf_ref, rb_ref, kb_ref, vb_ref, kkb_ref, abb_ref,
                 lwb_ref, of_ref, ob_ref, st_ref, *, pairs):
    @pl.when(pl.program_id(1) == 0)
    def _():
        st_ref[...] = jnp.zeros_like(st_ref)

    cl, hd, pw = SCAN_CHUNK, C_HEAD_DIM, 2 * C_HEAD_DIM
    n_sub = rf_ref.shape[0] // cl
    cat = lambda a, b: jnp.concatenate([a, b], axis=0)
    cut = lambda x: [x[:, p * pw:(p + 1) * pw] for p in range(pairs)]
    row = lax.broadcasted_iota(jnp.int32, (cl, cl), 0)
    col = lax.broadcasted_iota(jnp.int32, (cl, cl), 1)
    lane = lax.broadcasted_iota(jnp.int32, (cl, pw), 1)
    m1 = (lane < hd).astype(F32)
    m2 = 1.0 - m1
    r2 = lax.broadcasted_iota(jnp.int32, (2 * cl, 2 * cl), 0)
    c2 = lax.broadcasted_iota(jnp.int32, (2 * cl, 2 * cl), 1)
    tt, ss = r2 % cl, c2 % cl
    same = (r2 // cl) == (c2 // cl)
    eye = jnp.where(r2 == c2, 1.0, 0.0)
    ti = lax.broadcasted_iota(jnp.int32, (cl, 2 * cl), 0)
    si = lax.broadcasted_iota(jnp.int32, (cl, 2 * cl), 1) % cl
    every = range(2 * pairs)

    def chunk(sub, s0):
        kkg, rg, kd, bd, kc, bc, vs, decay, strict, incl, rows = ([] for _ in range(11))
        for z, refs in enumerate(((rf_ref, kf_ref, vf_ref, kkf_ref, abf_ref, lwf_ref),
                                  (rb_ref, kb_ref, vb_ref, kkb_ref, abb_ref, lwb_ref))):
            reverse = z == 1
            first = (n_sub - 1 - sub if reverse else sub) * cl
            rows.append(slice(first, first + cl))
            r, k, v, kk, ab, lw = (ref[rows[z], :] for ref in refs)
            tri = (col >= row) if reverse else (col <= row)
            c = _mm_exact_lhs(tri.astype(BF16), lw)
            total = c[0:1] if reverse else c[cl - 1:cl]
            e_out = jnp.exp(-c)
            e_last = jnp.exp(total - c)
            kkg += cut(kk * jnp.exp(c - lw))
            rg += cut(r * jnp.exp(c))
            kd += cut(k * e_out)
            bd += cut(ab * e_out)
            kc += cut(k * e_last)
            bc += cut(ab * e_last)
            vs += cut(v)
            decay += cut(jnp.exp(total))
            strict += [(ss > tt) if reverse else (ss < tt)] * pairs
            incl += [(si >= ti) if reverse else (si <= ti)] * pairs

        g1 = [_mm1(cat(kkg[p] * m1, rg[p] * m1), cat(bd[p], kd[p]), NT) for p in every]
        g2 = [_mm1(cat(kkg[p] * m2, rg[p] * m2), cat(kd[p], bd[p]), NT) for p in every]
        xs = [_mm1(cat(kkg[p], rg[p]), s0[p], NT) for p in every]
        gl = [jnp.where(strict[p], cat(g1[p][0:cl], g2[p][0:cl]), 0.0) for p in every]
        l_diag = [jnp.where(same, gl[p], 0.0) for p in every]
        l_anti = [jnp.where(same, 0.0, gl[p]) for p in every]
        lkv = [_mm1(l_anti[p], cat(vs[p] * m2, vs[p] * m1)) for p in every]

        inv = [eye - l_diag[p] for p in every]
        pwr = [_mm1(l_diag[p], l_diag[p]) for p in every]
        n_sq = int(math.log2(cl)) - 1
        for i in range(n_sq):
            inv = [inv[p] + _mm1(inv[p], pwr[p]) for p in every]
            if i + 1 < n_sq:
                pwr = [_mm1(pwr[p], pwr[p]) for p in every]

        rhs = [xs[p][0:cl] + lkv[p][0:cl] + lkv[p][cl:] for p in every]
        ust = [_mm1(inv[p], cat(rhs[p] * m1, rhs[p] * m2)) for p in every]
        u = [ust[p][0:cl] + ust[p][cl:] for p in every]
        o1 = [_mm1(jnp.where(incl[p], g1[p][cl:], 0.0), cat(-u[p] * m1, vs[p] * m1)) for p in every]
        o2 = [_mm1(jnp.where(incl[p], g2[p][cl:], 0.0), cat(vs[p] * m2, -u[p] * m2)) for p in every]
        upd = [_mm1(cat(vs[p], -u[p]), cat(kc[p], bc[p]), TN) for p in every]
        for p in every:
            z, q = divmod(p, pairs)
            o_ref = ob_ref if z else of_ref
            o_ref[rows[z], q * pw:(q + 1) * pw] = xs[p][cl:] + o1[p] + o2[p]
        return [s0[p] * decay[p] + jnp.where(same, upd[p], 0.0) for p in every]

    state = [st_ref[p // pairs, p % pairs] for p in every]
    for sub in range(n_sub):
        state = chunk(sub, state)
    for p in every:
        st_ref[p // pairs, p % pairs] = state[p]


def _rwkv_scan(r, lwf, lwb, k, v, kk, ab, lay):
    n, width = r.shape
    pairs = width // (2 * C_HEAD_DIM)
    steps = (lay.tc + lay.tl) // SCAN_BLOCK
    blk = lambda reverse: pl.BlockSpec(
        (SCAN_BLOCK, width), lambda b, s: (lay.scan_block(b, s, reverse), 0))
    out = jax.ShapeDtypeStruct((n, width), F32)
    return pl.pallas_call(
        functools.partial(_rwkv_kernel, pairs=pairs),
        grid=(lay.bsz, steps),
        in_specs=[blk(False)] * 6 + [blk(True)] * 6,
        out_specs=[blk(False), blk(True)],
        out_shape=[out, out],
        scratch_shapes=[pltpu.VMEM((2, pairs, 2 * C_HEAD_DIM, 2 * C_HEAD_DIM), F32)],
        compiler_params=_params("parallel", "arbitrary"),
        name="rwkv",
    )(r, k, v, kk, ab, lwf, r, k, v, kk, ab, lwb)


def _outproj_kernel(x_ref, ya_ref, hf_ref, hb_ref, og_ref, hg_ref, rf_ref, rb_ref, bonus_ref, gate_ref,
                    gng_ref, gnb_ref, avg_ref, w_ref, g1_ref, n2_ref, sc2_ref, sh2_ref, rw_ref, rbias_ref,
                    xo_ref, h2_ref, lg_ref, *, a_width, b_width):
    avg = avg_ref[...]
    scale2 = n2_ref[...] * (1.0 + sc2_ref[...])
    for r0 in range(0, x_ref.shape[0], ROW_BLOCK):
        rows = slice(r0, r0 + ROW_BLOCK)
        o = hf_ref[rows, :] + hb_ref[rows, :]
        og = og_ref[rows, :]
        yb = []
        for h in range(b_width // B_HEAD_DIM):
            lo, hi = h * B_HEAD_DIM, (h + 1) * B_HEAD_DIM
            yb.append(_rms(o[:, lo:hi]) * hg_ref[:, lo:hi] * _sigmoid(og[:, lo:hi]))
        yb = jnp.concatenate(yb, axis=1)

        rw = rf_ref[rows, :] + rb_ref[rows, :]
        xc = rw - _head_sums(rw, avg)
        var = _head_sums(xc * xc, avg)
        yc = ((xc * lax.rsqrt(var + RWKV_GN_EPS) * gng_ref[...] + gnb_ref[...] + bonus_ref[rows, :])
              * gate_ref[rows, :])

        y = (_mm1(ya_ref[rows, :], w_ref[0:a_width]) + _mm1(yb, w_ref[a_width:a_width + b_width])
             + _mm1(yc, w_ref[a_width + b_width:]))
        xn = x_ref[rows, :] + g1_ref[...] * y
        xo_ref[rows, :] = xn
        h2 = _rms(xn) * scale2 + sh2_ref[...]
        h2_ref[rows, :] = h2.astype(BF16)
        lg_ref[:, rows] = _mm3(rw_ref[...], h2, NT) + rbias_ref[...]


def _outproj(x_all, ya, hf, hb, pb, hgrn_g, rf, rb, bonus, gate, gn_g, gn_b, w_out_bf16, mods, layer,
             norm2_g, router_wt, router_b_col, lay, tile_off, n_tiles):
    n, d = x_all.shape
    a_width = ya.shape[1]
    b_width = hf.shape[1]
    c_width = rf.shape[1]
    tm = TM_TOKEN
    avg = _head_block_diag(C_HEAD_DIM, 1.0 / C_HEAD_DIM)
    assert c_width % MXU_TILE == 0
    tok = lambda w, comp=0, off=tile_off: pl.BlockSpec((tm, w), lambda i: (i + off, comp))
    full = lambda a: pl.BlockSpec(a.shape, lambda i: (0,) * a.ndim)
    row = lambda a: a.reshape(1, -1)
    mod = lambda part: _mod_spec(lay, tm, layer, part, d, tile_off, 0)
    hg, gg, gb, n2 = row(hgrn_g), row(gn_g), row(gn_b), row(norm2_g)
    rows = n_tiles * tm
    return pl.pallas_call(
        functools.partial(_outproj_kernel, a_width=a_width, b_width=b_width),
        grid=(n_tiles,),
        in_specs=[tok(d), tok(a_width, off=0), tok(b_width), tok(b_width), tok(b_width, 4), full(hg),
                  tok(c_width), tok(c_width), tok(c_width), tok(c_width), full(gg), full(gb), full(avg),
                  full(w_out_bf16), mod(2), full(n2), mod(4), mod(3), full(router_wt), full(router_b_col)],
        out_specs=[tok(d, off=0), tok(d, off=0), pl.BlockSpec((N_EXPERTS, tm), lambda i: (0, i))],
        out_shape=[jax.ShapeDtypeStruct((rows, d), F32), jax.ShapeDtypeStruct((rows, d), BF16),
                   jax.ShapeDtypeStruct((N_EXPERTS, rows), F32)],
        compiler_params=_params("parallel"),
        name="outproj",
    )(x_all, ya, hf, hb, pb, hg, rf, rb, bonus, gate, gg, gb, avg, w_out_bf16, mods, n2, mods, mods,
      router_wt, router_b_col)


def _route(logits):
    per_group = N_EXPERTS // N_EXPERT_GROUPS
    assert per_group == 4
    expert = lax.broadcasted_iota(jnp.int32, logits.shape, 0)
    ex = jnp.exp(logits - jnp.max(logits, axis=0, keepdims=True))
    p = ex / jnp.sum(ex, axis=0, keepdims=True)

    def top2_sum(a, b, c, d):
        h1, l1, h2, l2 = jnp.maximum(a, b), jnp.minimum(a, b), jnp.maximum(c, d), jnp.minimum(c, d)
        return jnp.maximum(h1, h2) + jnp.maximum(jnp.minimum(h1, h2), jnp.maximum(l1, l2))

    score = [top2_sum(*(p[g * per_group + i:g * per_group + i + 1] for i in range(per_group)))
             for g in range(N_EXPERT_GROUPS)]
    best, sel = score[0], jnp.zeros(score[0].shape, jnp.int32)
    for g in range(1, N_EXPERT_GROUPS):
        better = score[g] > best
        best = jnp.where(better, score[g], best)
        sel = jnp.where(better, g, sel)
    pm = jnp.where(expert // per_group == sel, p, -1.0)

    def first_max(vals):
        top = jnp.max(vals, axis=0, keepdims=True)
        return top, jnp.min(jnp.where(vals == top, expert, N_EXPERTS), axis=0, keepdims=True)

    p1, i1 = first_max(pm)
    p2, i2 = first_max(jnp.where(expert == i1, -2.0, pm))
    tot = p1 + p2
    gates = jnp.where(expert == i1, p1 / tot, 0.0) + jnp.where(expert == i2, p2 / tot, 0.0)
    return gates, (expert == i1) | (expert == i2)


def _moe_kernel(h_ref, lg_ref, x_ref, g2_ref, fg_ref, wgu_ref, wd_ref, o_ref, gate_ref, rank_ref,
                *, d_expert, rows, final_norm):
    e = pl.program_id(1)
    tm = h_ref.shape[0]

    @pl.when(e == 0)
    def _():
        o_ref[...] = x_ref[...]
        gates, chosen = _route(lg_ref[...])
        lane = lax.broadcasted_iota(jnp.int32, chosen.shape, 1)
        count = jnp.where(chosen, 1.0, 0.0)
        step = 1
        while step < tm:
            count = count + jnp.where(lane >= step, pltpu.roll(count, step, axis=1), 0.0)
            step *= 2
        gate_ref[...] = gates
        rank_ref[...] = jnp.where(chosen, count - 1.0, -1.0)

    g_row = gate_ref[pl.ds(e, 1), :]
    rank_row = rank_ref[pl.ds(e, 1), :]
    n_sel = jnp.max(rank_row, axis=1, keepdims=True)[0, 0].astype(jnp.int32) + 1

    def one_pass(j):
        slot = (lax.broadcasted_iota(jnp.int32, (rows, tm), 0) + j * rows).astype(F32)
        hit = rank_row == slot
        pick = jnp.where(hit, 1.0, 0.0).astype(BF16)
        hg = _dot(pick, h_ref[...]).astype(BF16)
        gu = _dot(hg, wgu_ref[...])
        gt, up = gu[:, :d_expert], gu[:, d_expert:]
        y = _mm1(gt * _sigmoid(gt) * up, wd_ref[...])
        gcol = jnp.sum(jnp.where(hit, g_row, 0.0), axis=1, keepdims=True)
        o_ref[...] += _dot(pick, (y * gcol * g2_ref[...]).astype(BF16), TN)
        return j + 1

    lax.while_loop(lambda j: j * rows < n_sel, one_pass, jnp.int32(0))

    if final_norm:
        @pl.when(e == pl.num_programs(1) - 1)
        def _():
            o_ref[...] = _rms(o_ref[...]) * fg_ref[...]


def _moe(h2, logits_t, x_mid, mods, layer, wgu_bf16, wd_bf16, final_g, lay, row_off, final_norm):
    n, d = h2.shape
    n_exp, _, two_de = wgu_bf16.shape
    tm = MOE_TM
    assert n % tm == 0 and row_off % tm == 0 and lay.n_ctx % tm == 0 and lay.tl % tm == 0
    once = pl.Buffered(1)
    tok = pl.BlockSpec((tm, d), lambda i, e: (i, 0), pipeline_mode=once)
    return pl.pallas_call(
        functools.partial(_moe_kernel, d_expert=two_de // 2, rows=MOE_ROWS, final_norm=final_norm),
        grid=(n // tm, n_exp),
        in_specs=[tok, pl.BlockSpec((n_exp, tm), lambda i, e: (0, i)), tok,
                  _mod_spec(lay, tm, layer, 5, d, row_off // tm, 0),
                  pl.BlockSpec((1, d), lambda i, e: (0, 0)),
                  pl.BlockSpec((None, d, two_de), lambda i, e: (e, 0, 0)),
                  pl.BlockSpec((None, two_de // 2, d), lambda i, e: (e, 0, 0))],
        out_specs=tok,
        out_shape=jax.ShapeDtypeStruct((n, d), F32),
        scratch_shapes=[pltpu.VMEM((n_exp, tm), F32), pltpu.VMEM((n_exp, tm), F32)],
        compiler_params=_params("parallel", "arbitrary"),
        name="moe",
    )(h2, logits_t, x_mid, mods, final_g.reshape(1, d), wgu_bf16, wd_bf16)


def kernel(x, c, ctx, c_ctx, ada_w, ada_b, norm1_g, norm2_g, w_in, w_out, gmlp_norm_g, gmlp_ws, gmlp_b,
           hgrn_lb_logits, hgrn_norm_g, rwkv_mu, rwkv_w0, rwkv_w_up, rwkv_a0, rwkv_a_up, rwkv_g_up,
           rwkv_k_k, rwkv_k_a, rwkv_r_k, rwkv_gn_g, rwkv_gn_b, router_w, router_b, moe_w_gate_up,
           moe_w_down, final_norm_g):
    bsz, tl, d = x.shape
    tc = ctx.shape[1]
    depth = ada_w.shape[0]
    lay = _Layout(bsz, tc, tl)
    a_cols = 2 * gmlp_norm_g.shape[1]
    b_cols = 5 * hgrn_norm_g.shape[1]
    c_width = rwkv_a0.shape[1]
    assert lay.n_ctx % TM_MATMUL == 0 and tl % TM_MATMUL == 0 and tc % SCAN_BLOCK == 0

    mod_rows = -(-(bsz + 1) // 8) * 8
    c_rows = jnp.zeros((mod_rows, d), F32).at[:bsz].set(c).at[bsz].set(c_ctx)
    mods = _adaln(c_rows, ada_w, ada_b).reshape(depth, mod_rows, 6, 1, d)

    lbs = jnp.cumsum(jax.nn.softmax(hgrn_lb_logits.astype(F32), axis=1), axis=1)
    lbs = lbs - lbs[:, :1]
    w_in_bf16 = w_in.astype(BF16)
    w_out_bf16 = w_out.astype(BF16)
    wgu_bf16 = moe_w_gate_up.astype(BF16)
    wd_bf16 = moe_w_down.astype(BF16)
    router_wt = router_w.T
    router_b_col = router_b.reshape(N_EXPERTS, 1)

    x_all = jnp.concatenate([ctx.reshape(lay.n_ctx, d), x.reshape(lay.n_lat, d)], axis=0)
    ctx_tiles = lay.n_ctx // TM_TOKEN
    all_tiles = lay.n // TM_TOKEN

    for l in range(depth):
        last = l == depth - 1
        tile_off = ctx_tiles if last else 0
        n_tiles = all_tiles - tile_off
        w_l = w_in_bf16[l]
        pa = _inproj(x_all, norm1_g[l], mods, l, w_l[:, :a_cols], a_cols, lay)
        pb = _inproj(x_all, norm1_g[l], mods, l, w_l[:, a_cols:a_cols + b_cols], b_cols // 3, lay)
        pc = _inproj(x_all, norm1_g[l], mods, l, w_l[:, a_cols + b_cols:], (w_l.shape[1] - a_cols - b_cols) // 2, lay)

        ya = _gmlp(pa, gmlp_norm_g[l], gmlp_ws[l], gmlp_b[l], tile_off, n_tiles)
        hf, hb = _hgrn(pb, lbs[:, l], lay)

        w_lora = jnp.zeros((DECAY_LORA + AAA_LORA, 3 * c_width), F32)
        w_lora = w_lora.at[:DECAY_LORA, :c_width].set(rwkv_w_up[l, 0])
        w_lora = w_lora.at[:DECAY_LORA, c_width:2 * c_width].set(rwkv_w_up[l, 1])
        w_lora = w_lora.at[DECAY_LORA:, 2 * c_width:].set(rwkv_a_up[l])
        r, lwf, lwb, k2, v, kk, ab, gate, bonus = _rwkv_prep(
            pc, lay, rwkv_mu[l], w_lora, rwkv_g_up[l], rwkv_w0[l], rwkv_a0[l], rwkv_k_k[l], rwkv_k_a[l],
            rwkv_r_k[l])
        rf, rb = _rwkv_scan(r, lwf, lwb, k2, v, kk, ab, lay)

        x_mid, h2, logits_t = _outproj(
            x_all, ya, hf, hb, pb, hgrn_norm_g[l], rf, rb, bonus, gate, rwkv_gn_g[l], rwkv_gn_b[l],
            w_out_bf16[l], mods, l, norm2_g[l], router_wt, router_b_col, lay, tile_off, n_tiles)
        x_all = _moe(h2, logits_t, x_mid, mods, l, wgu_bf16[l], wd_bf16[l], final_norm_g, lay,
                     tile_off * TM_TOKEN, final_norm=last)
    return x_all.reshape(bsz, tl, d)
```
